```python
import jax, jax.numpy as jnp
from jax import lax
import numpy as np

D_MODEL = 2048
BATCH = 4
SEQ = 4096
DEPTH = 1

N_META = 16
N_Q_HEADS = 16
N_KV_HEADS = 4
HEAD_DIM = 128
Q_GROUP = N_Q_HEADS // N_KV_HEADS
ROT_DIM = HEAD_DIM // 4
ROPE_THETA = 500000.0
WINDOW = 128
ATTN_BLOCK = 128
D_LRU = D_MODEL
LRU_BLOCKS = 16
LRU_BW = D_LRU // LRU_BLOCKS
CONV_W = 4
RGLRU_C = 8.0
N_EXPERTS = 256
TOP_K = 8
N_GROUPS = 8
TOPK_GROUPS = 4
D_EXPERT = 512
ROUTED_SCALE = 2.5
EXPERT_BLOCK = 128
DN_ALPHA = (2.0 * DEPTH) ** 0.25
DN_BETA = (8.0 * DEPTH) ** -0.25
LN_EPS = 1e-5
NEG_INF = -1e30
Q_DIM = N_Q_HEADS * HEAD_DIM
KV_DIM = N_KV_HEADS * HEAD_DIM
IN_SPLITS = (Q_DIM, KV_DIM, KV_DIM, D_LRU, D_LRU, D_MODEL, D_MODEL)
IN_DIM = sum(IN_SPLITS)
IN_OFFSETS = tuple(int(v) for v in np.cumsum(IN_SPLITS)[:-1])

kernel_name = "hybrid_rglru_swa_sink_moe_block"


def _layernorm(x, g, b):
    xf = x.astype(jnp.float32)
    mu = jnp.mean(xf, axis=-1, keepdims=True)
    var = jnp.mean(jnp.square(xf - mu), axis=-1, keepdims=True)
    y = (xf - mu) * lax.rsqrt(var + LN_EPS) * g.astype(jnp.float32) + b.astype(jnp.float32)
    return y.astype(x.dtype)


def _rope_partial(x, pos):
    half = ROT_DIM // 2
    inv_freq = ROPE_THETA ** (-jnp.arange(half, dtype=jnp.float32) / half)
    ang = pos.astype(jnp.float32)[:, None] * inv_freq[None, :]
    cos = jnp.cos(ang)[None, :, None, :]
    sin = jnp.sin(ang)[None, :, None, :]
    x1 = x[..., :half].astype(jnp.float32)
    x2 = x[..., half:ROT_DIM].astype(jnp.float32)
    rot = jnp.concatenate([x1 * cos - x2 * sin, x2 * cos + x1 * sin], axis=-1)
    return jnp.concatenate([rot.astype(x.dtype), x[..., ROT_DIM:]], axis=-1)


def _swa_with_sinks(q, k, v, sinks):
    B, T = q.shape[0], q.shape[1]
    pad = ATTN_BLOCK - N_META
    L = T + pad
    nb = L // ATTN_BLOCK
    padf = lambda a: jnp.pad(a, ((0, 0), (pad, 0), (0, 0), (0, 0)))
    qb = padf(q).reshape(B, nb, ATTN_BLOCK, N_KV_HEADS, Q_GROUP, HEAD_DIM)
    kb = padf(k).reshape(B, nb, ATTN_BLOCK, N_KV_HEADS, HEAD_DIM)
    vb = padf(v).reshape(B, nb, ATTN_BLOCK, N_KV_HEADS, HEAD_DIM)
    zk = jnp.zeros_like(kb[:, :1])
    k_band = jnp.concatenate([jnp.concatenate([zk, kb[:, :-1]], axis=1), kb], axis=2)
    v_band = jnp.concatenate([jnp.concatenate([zk, vb[:, :-1]], axis=1), vb], axis=2)
    k_meta = k[:, :N_META]
    v_meta = v[:, :N_META]
    scale = HEAD_DIM ** -0.5
    s_band = jnp.einsum('bnqhgd,bnkhd->bnhgqk', qb, k_band).astype(jnp.float32) * scale
    s_meta = jnp.einsum('bnqhgd,bmhd->bnhgqm', qb, k_meta).astype(jnp.float32) * scale
    q_pos = jnp.arange(L).reshape(nb, ATTN_BLOCK)
    k_pos = q_pos[:, :1] - ATTN_BLOCK + jnp.arange(2 * ATTN_BLOCK)[None, :]
    diff = q_pos[:, :, None] - k_pos[:, None, :]
    band_ok = (diff >= 0) & (diff < WINDOW) & (k_pos[:, None, :] >= ATTN_BLOCK)
    meta_pos = pad + jnp.arange(N_META)
    meta_ok = meta_pos[None, None, :] <= q_pos[:, :, None]
    s_band = jnp.where(band_ok[None, :, None, None], s_band, NEG_INF)
    s_meta = jnp.where(meta_ok[None, :, None, None], s_meta, NEG_INF)
    sink = sinks.astype(jnp.float32).reshape(N_KV_HEADS, Q_GROUP)[None, None, :, :, None, None]
    m = jnp.maximum(jnp.maximum(jnp.max(s_band, -1, keepdims=True), jnp.max(s_meta, -1, keepdims=True)), sink)
    p_band = jnp.exp(s_band - m)
    p_meta = jnp.exp(s_meta - m)
    denom = jnp.sum(p_band, -1, keepdims=True) + jnp.sum(p_meta, -1, keepdims=True) + jnp.exp(sink - m)
    p_band = p_band / denom
    p_meta = p_meta / denom
    o = (jnp.einsum('bnhgqk,bnkhd->bnqhgd', p_band, v_band.astype(jnp.float32))
         + jnp.einsum('bnhgqm,bmhd->bnqhgd', p_meta, v_meta.astype(jnp.float32)))
    o = o.reshape(B, L, Q_DIM)[:, pad:]
    return o.astype(q.dtype)


def _causal_conv(x, w, b):
    T = x.shape[1]
    xp = jnp.pad(x, ((0, 0), (CONV_W - 1, 0), (0, 0)))
    y = b
    for j in range(CONV_W):
        y = y + xp[:, j:j + T] * w[j]
    return y


def _rglru(x, w_a, b_a, w_x, b_x, lam):
    B, T = x.shape[0], x.shape[1]
    xb = x.reshape(B, T, LRU_BLOCKS, LRU_BW)
    r = jax.nn.sigmoid(jnp.einsum('btni,nij->btnj', xb, w_a) + b_a).reshape(B, T, D_LRU)
    i = jax.nn.sigmoid(jnp.einsum('btni,nij->btnj', xb, w_x) + b_x).reshape(B, T, D_LRU)
    log_a = -RGLRU_C * r.astype(jnp.float32) * jax.nn.softplus(-lam.astype(jnp.float32))
    a = jnp.exp(log_a)
    u = jnp.sqrt(-jnp.expm1(2.0 * log_a)) * (i * x).astype(jnp.float32)

    def combine(c1, c2):
        a1, b1 = c1
        a2, b2 = c2
        return a1 * a2, a2 * b1 + b2

    _, h = lax.associative_scan(combine, (a, u), axis=1)
    return h.astype(x.dtype)


def _moe(h, w_router, router_bias, w_gate_e, w_up_e, w_down_e, w_gate_s, w_up_s, w_down_s):
    B, T, D = h.shape
    xt = h.reshape(B * T, D)
    N = xt.shape[0]
    scores = jax.nn.sigmoid((xt @ w_router).astype(jnp.float32))
    sel = scores + router_bias.astype(jnp.float32)
    grp = sel.reshape(N, N_GROUPS, N_EXPERTS // N_GROUPS)
    grp_score = jnp.sum(lax.top_k(grp, 2)[0], axis=-1)
    _, top_groups = lax.top_k(grp_score, TOPK_GROUPS)
    group_mask = jnp.sum(jax.nn.one_hot(top_groups, N_GROUPS, dtype=jnp.float32), axis=1) > 0
    expert_mask = jnp.repeat(group_mask, N_EXPERTS // N_GROUPS, axis=1)
    _, topi = lax.top_k(jnp.where(expert_mask, sel, -jnp.inf), TOP_K)
    topw = jnp.take_along_axis(scores, topi, axis=1)
    topw = topw / jnp.sum(topw, axis=-1, keepdims=True) * ROUTED_SCALE

    A = N * TOP_K
    flat_e = topi.reshape(-1)
    flat_tok = jnp.repeat(jnp.arange(N, dtype=jnp.int32), TOP_K)
    flat_w = topw.reshape(-1)
    order = jnp.argsort(flat_e)
    e_sorted, tok_sorted, w_sorted = flat_e[order], flat_tok[order], flat_w[order]
    counts = jnp.bincount(flat_e, length=N_EXPERTS)
    padded = (counts + EXPERT_BLOCK - 1) // EXPERT_BLOCK * EXPERT_BLOCK
    grp_start = jnp.cumsum(counts) - counts
    pad_end = jnp.cumsum(padded)
    pad_start = pad_end - padded
    dest = pad_start[e_sorted] + (jnp.arange(A) - grp_start[e_sorted])
    n_blocks = -(-(A + N_EXPERTS * (EXPERT_BLOCK - 1)) // EXPERT_BLOCK)
    P = n_blocks * EXPERT_BLOCK
    row_tok = jnp.full((P,), N, jnp.int32).at[dest].set(tok_sorted)
    row_w = jnp.zeros((P,), xt.dtype).at[dest].set(w_sorted.astype(xt.dtype))
    blk_e = jnp.minimum(jnp.searchsorted(pad_end, jnp.arange(n_blocks) * EXPERT_BLOCK, side='right'),
                        N_EXPERTS - 1)
    x_pad = jnp.concatenate([xt, jnp.zeros((1, D), xt.dtype)], axis=0)

    def expert_block(acc, blk):
        tok, wt, e = blk
        xb = x_pad[tok]
        hid = jax.nn.silu(xb @ w_gate_e[e]) * (xb @ w_up_e[e])
        out = (hid @ w_down_e[e]) * wt[:, None]
        return acc.at[tok].add(out), None

    acc0 = jnp.zeros((N + 1, D), xt.dtype)
    routed, _ = lax.scan(expert_block, acc0,
                         (row_tok.reshape(n_blocks, EXPERT_BLOCK), row_w.reshape(n_blocks, EXPERT_BLOCK), blk_e))
    shared = (jax.nn.silu(xt @ w_gate_s) * (xt @ w_up_s)) @ w_down_s
    return (routed[:N] + shared).reshape(B, T, D)


def setup_inputs(seed: int = 0) -> dict:
    key = jax.random.key(seed)
    ks = jax.random.split(key, 28)
    f32 = jnp.float32
    L = DEPTH

    def nrm(k, shape, scale):
        return jax.random.normal(k, shape, f32) * scale

    a_c = jax.random.uniform(ks[9], (L, D_LRU), f32, 0.9, 0.999)
    a_base = a_c ** (1.0 / RGLRU_C)
    rg_lambda = jnp.log(a_base) - jnp.log1p(-a_base)
    return {
        "x": nrm(ks[0], (BATCH, SEQ, D_MODEL), 1.0),
        "meta_tokens": nrm(ks[1], (N_META, D_MODEL), 1.0),
        "w_in": nrm(ks[2], (L, D_MODEL, IN_DIM), D_MODEL ** -0.5),
        "conv_w": nrm(ks[3], (L, CONV_W, D_LRU), CONV_W ** -0.5),
        "conv_b": nrm(ks[4], (L, D_LRU), 0.01),
        "w_rg_a": nrm(ks[5], (L, LRU_BLOCKS, LRU_BW, LRU_BW), LRU_BW ** -0.5),
        "b_rg_a": nrm(ks[6], (L, LRU_BLOCKS, LRU_BW), 0.01),
        "w_rg_x": nrm(ks[7], (L, LRU_BLOCKS, LRU_BW, LRU_BW), LRU_BW ** -0.5),
        "b_rg_x": nrm(ks[8], (L, LRU_BLOCKS, LRU_BW), 0.01),
        "rg_lambda": rg_lambda,
        "attn_sinks": nrm(ks[10], (L, N_Q_HEADS), 0.5),
        "w_o_attn": nrm(ks[11], (L, Q_DIM, D_MODEL), Q_DIM ** -0.5),
        "w_o_lru": nrm(ks[12], (L, D_LRU, D_MODEL), D_LRU ** -0.5),
        "w_out": nrm(ks[13], (L, D_MODEL, D_MODEL), D_MODEL ** -0.5 * DN_BETA),
        "ln1_g": 1.0 + nrm(ks[14], (L, D_MODEL), 0.01),
        "ln1_b": nrm(ks[15], (L, D_MODEL), 0.01),
        "w_router": nrm(ks[16], (L, D_MODEL, N_EXPERTS), D_MODEL ** -0.5),
        "router_bias": nrm(ks[17], (L, N_EXPERTS), 0.01),
        "w_gate_e": nrm(ks[18], (L, N_EXPERTS, D_MODEL, D_EXPERT), D_MODEL ** -0.5),
        "w_up_e": nrm(ks[19], (L, N_EXPERTS, D_MODEL, D_EXPERT), D_MODEL ** -0.5),
        "w_down_e": nrm(ks[20], (L, N_EXPERTS, D_EXPERT, D_MODEL), D_EXPERT ** -0.5 * DN_BETA),
        "w_gate_s": nrm(ks[21], (L, D_MODEL, D_EXPERT), D_MODEL ** -0.5),
        "w_up_s": nrm(ks[22], (L, D_MODEL, D_EXPERT), D_MODEL ** -0.5),
        "w_down_s": nrm(ks[23], (L, D_EXPERT, D_MODEL), D_EXPERT ** -0.5 * DN_BETA),
        "ln2_g": 1.0 + nrm(ks[24], (L, D_MODEL), 0.01),
        "ln2_b": nrm(ks[25], (L, D_MODEL), 0.01),
    }


def reference(x, meta_tokens, w_in, conv_w, conv_b, w_rg_a, b_rg_a, w_rg_x, b_rg_x, rg_lambda,
              attn_sinks, w_o_attn, w_o_lru, w_out, ln1_g, ln1_b, w_router, router_bias,
              w_gate_e, w_up_e, w_down_e, w_gate_s, w_up_s, w_down_s, ln2_g, ln2_b):
    B = x.shape[0]
    meta = jnp.broadcast_to(meta_tokens.astype(x.dtype)[None], (B, N_META, D_MODEL))
    h = jnp.concatenate([meta, x], axis=1)
    T = h.shape[1]
    pos = jnp.arange(T)
    for l in range(DEPTH):
        proj = h @ w_in[l]
        q, k, v, xr, xg, g_attn, g_lru = jnp.split(proj, IN_OFFSETS, axis=-1)
        q = _rope_partial(q.reshape(B, T, N_Q_HEADS, HEAD_DIM), pos)
        k = _rope_partial(k.reshape(B, T, N_KV_HEADS, HEAD_DIM), pos)
        v = v.reshape(B, T, N_KV_HEADS, HEAD_DIM)
        y_attn = _swa_with_sinks(q, k, v, attn_sinks[l]) @ w_o_attn[l]
        xr = _causal_conv(xr, conv_w[l], conv_b[l])
        y_lru = (_rglru(xr, w_rg_a[l], b_rg_a[l], w_rg_x[l], b_rg_x[l], rg_lambda[l]) * jax.nn.gelu(xg)) @ w_o_lru[l]
        mix = (jax.nn.sigmoid(g_attn) * y_attn + jax.nn.sigmoid(g_lru) * y_lru) @ w_out[l]
        h = _layernorm(DN_ALPHA * h + mix, ln1_g[l], ln1_b[l])
        ffn = _moe(h, w_router[l], router_bias[l], w_gate_e[l], w_up_e[l], w_down_e[l],
                   w_gate_s[l], w_up_s[l], w_down_s[l])
        h = _layernorm(DN_ALPHA * h + ffn, ln2_g[l], ln2_b[l])
    return h[:, N_META:]
```

```python
import functools

import jax
import jax.numpy as jnp
import numpy as np
from jax import lax
from jax.experimental import pallas as pl
from jax.experimental.pallas import tpu as pltpu

F32 = jnp.float32
BF16 = jnp.bfloat16
U32 = jnp.uint32
I32 = jnp.int32

D_MODEL = 2048
N_META = 16
N_Q_HEADS = 16
N_KV_HEADS = 4
HEAD_DIM = 128
Q_GROUP = N_Q_HEADS // N_KV_HEADS
ROT_DIM = HEAD_DIM // 4
ROPE_THETA = 500000.0
ATTN_BLOCK = 128
D_LRU = D_MODEL
LRU_BLOCKS = 16
LRU_BW = D_LRU // LRU_BLOCKS
CONV_W = 4
RGLRU_C = 8.0
N_EXPERTS = 256
TOP_K = 8
N_GROUPS = 8
GROUP_SIZE = N_EXPERTS // N_GROUPS
TOPK_GROUPS = 4
D_EXPERT = 512
ROUTED_SCALE = 2.5
EXPERT_BLOCK = 128
DN_ALPHA = 2.0 ** 0.25
LN_EPS = 1e-5
NEG_INF = -1e30
KV_DIM = N_KV_HEADS * HEAD_DIM
IN_DIM = 5 * D_MODEL + 2 * KV_DIM
K_COLBLK = 5 * D_MODEL // KV_DIM
V_COLBLK = K_COLBLK + 1
HALF_PACK = D_MODEL // 2
VMEM_LIMIT = 50 * 1024 * 1024


def _cparams(sem):
    return pltpu.CompilerParams(dimension_semantics=sem, vmem_limit_bytes=VMEM_LIMIT)


def _mm_kernel(a_ref, b_ref, o_ref):
    o_ref[...] = jnp.dot(a_ref[...], b_ref[...], preferred_element_type=F32).astype(o_ref.dtype)


def _matmul(a, b, out_dtype, tm, tn):
    m, k = a.shape
    n = b.shape[1]
    return pl.pallas_call(
        _mm_kernel,
        out_shape=jax.ShapeDtypeStruct((m, n), out_dtype),
        grid=(n // tn, m // tm),
        in_specs=[pl.BlockSpec((tm, k), lambda j, i: (i, 0)),
                  pl.BlockSpec((k, tn), lambda j, i: (0, j))],
        out_specs=pl.BlockSpec((tm, tn), lambda j, i: (i, j)),
        compiler_params=_cparams(("parallel", "parallel")),
        name="dense_matmul",
    )(a, b)


def _rope(x, c, s1, s2):
    return x * c + pltpu.roll(x, HEAD_DIM - ROT_DIM // 2, 1) * s1 + pltpu.roll(x, ROT_DIM // 2, 1) * s2


def _attn_kernel(sink_ref, q_ref, kc_ref, kp_ref, vc_ref, vp_ref, km_ref, vm_ref,
                 cq_ref, s1q_ref, s2q_ref, cp_ref, s1p_ref, s2p_ref, cm_ref, s1m_ref, s2m_ref,
                 o_ref, *, tq):
    n = pl.program_id(1)
    nsub = tq // ATTN_BLOCK
    scale = HEAD_DIM ** -0.5
    cq, s1q, s2q = cq_ref[...], s1q_ref[...], s2q_ref[...]
    cp, s1p, s2p = cp_ref[...], s1p_ref[...], s2p_ref[...]
    cm, s1m, s2m = cm_ref[...], s1m_ref[...], s2m_ref[...]
    rows = Q_GROUP * ATTN_BLOCK
    r_idx = lax.broadcasted_iota(I32, (rows, ATTN_BLOCK), 0) % ATTN_BLOCK
    c_idx = lax.broadcasted_iota(I32, (rows, ATTN_BLOCK), 1)
    cur_ok = c_idx <= r_idx
    prev_ok = c_idx > r_idx
    dn = (((1,), (1,)), ((), ()))
    for g in range(N_KV_HEADS):
        hs = slice(g * HEAD_DIM, (g + 1) * HEAD_DIM)
        k_g = _rope(kc_ref[:, hs].astype(F32), cq, s1q, s2q).astype(BF16)
        kp_g = _rope(kp_ref[:, hs].astype(F32), cp, s1p, s2p).astype(BF16)
        km_g = _rope(km_ref[:, hs].astype(F32), cm, s1m, s2m).astype(BF16)
        v_g = vc_ref[:, hs]
        vp_g = vp_ref[:, hs]
        vm_g = vm_ref[:, hs]
        sink = jnp.concatenate(
            [jnp.full((ATTN_BLOCK, 1), sink_ref[g * Q_GROUP + j], F32) for j in range(Q_GROUP)], axis=0)
        for sub in range(nsub):
            rs = slice(sub * ATTN_BLOCK, (sub + 1) * ATTN_BLOCK)
            q_st = jnp.concatenate(
                [_rope(q_ref[rs, (g * Q_GROUP + j) * HEAD_DIM:(g * Q_GROUP + j + 1) * HEAD_DIM].astype(F32),
                       cq[rs], s1q[rs], s2q[rs]) for j in range(Q_GROUP)], axis=0).astype(BF16)
            if sub == 0:
                k_prev, v_prev = kp_g, vp_g
                p_ok = (c_idx - r_idx) > jnp.where(n > 0, 0, ATTN_BLOCK)
            else:
                ps = slice((sub - 1) * ATTN_BLOCK, sub * ATTN_BLOCK)
                k_prev, v_prev = k_g[ps], v_g[ps]
                p_ok = prev_ok
            s_p = lax.dot_general(q_st, k_prev, dn, preferred_element_type=F32) * scale
            s_c = lax.dot_general(q_st, k_g[rs], dn, preferred_element_type=F32) * scale
            s_m = lax.dot_general(q_st, km_g, dn, preferred_element_type=F32) * scale
            s_p = jnp.where(p_ok, s_p, NEG_INF)
            s_c = jnp.where(cur_ok, s_c, NEG_INF)
            m = jnp.maximum(jnp.maximum(jnp.max(s_p, axis=1, keepdims=True), jnp.max(s_c, axis=1, keepdims=True)),
                            jnp.maximum(jnp.max(s_m, axis=1, keepdims=True), sink))
            p_p = jnp.exp(s_p - m)
            p_c = jnp.exp(s_c - m)
            p_m = jnp.exp(s_m - m)
            den = (jnp.sum(p_p, axis=1, keepdims=True) + jnp.sum(p_c, axis=1, keepdims=True)
                   + jnp.sum(p_m, axis=1, keepdims=True) + jnp.exp(sink - m))
            o = (jnp.dot(p_p.astype(BF16), v_prev, preferred_element_type=F32)
                 + jnp.dot(p_c.astype(BF16), v_g[rs], preferred_element_type=F32)
                 + jnp.dot(p_m.astype(BF16), vm_g, preferred_element_type=F32)) / den
            for j in range(Q_GROUP):
                h = g * Q_GROUP + j
                o_ref[rs, h * HEAD_DIM:(h + 1) * HEAD_DIM] = o[j * ATTN_BLOCK:(j + 1) * ATTN_BLOCK].astype(BF16)


def _attention(proj, projm, sinks, tabs_real, tabs_meta, batch, seq, tq=512):
    n_tok = batch * seq
    nq = seq // tq
    sub_per = tq // ATTN_BLOCK

    def prev_blk(b, n):
        return jnp.maximum(b * (seq // ATTN_BLOCK) + n * sub_per - 1, 0)

    tab_spec = pl.BlockSpec((tq, HEAD_DIM), lambda b, n, s: (n, 0))
    tabp_spec = pl.BlockSpec((ATTN_BLOCK, HEAD_DIM), lambda b, n, s: (jnp.maximum(n * sub_per - 1, 0), 0))
    tabm_spec = pl.BlockSpec((N_META, HEAD_DIM), lambda b, n, s: (0, 0))
    in_specs = [
        pl.BlockSpec((tq, D_MODEL), lambda b, n, s: (b * nq + n, 0)),
        pl.BlockSpec((tq, KV_DIM), lambda b, n, s: (b * nq + n, K_COLBLK)),
        pl.BlockSpec((ATTN_BLOCK, KV_DIM), lambda b, n, s: (prev_blk(b, n), K_COLBLK)),
        pl.BlockSpec((tq, KV_DIM), lambda b, n, s: (b * nq + n, V_COLBLK)),
        pl.BlockSpec((ATTN_BLOCK, KV_DIM), lambda b, n, s: (prev_blk(b, n), V_COLBLK)),
        pl.BlockSpec((N_META, KV_DIM), lambda b, n, s: (0, K_COLBLK)),
        pl.BlockSpec((N_META, KV_DIM), lambda b, n, s: (0, V_COLBLK)),
        tab_spec, tab_spec, tab_spec, tabp_spec, tabp_spec, tabp_spec, tabm_spec, tabm_spec, tabm_spec,
    ]
    grid_spec = pltpu.PrefetchScalarGridSpec(
        num_scalar_prefetch=1, grid=(batch, nq), in_specs=in_specs,
        out_specs=pl.BlockSpec((tq, D_MODEL), lambda b, n, s: (b * nq + n, 0)))
    return pl.pallas_call(
        functools.partial(_attn_kernel, tq=tq),
        out_shape=jax.ShapeDtypeStruct((n_tok, D_MODEL), BF16),
        grid_spec=grid_spec,
        compiler_params=_cparams(("parallel", "parallel")),
        name="swa_attention",
    )(sinks, proj, proj, proj, proj, proj, projm, projm,
      tabs_real[0], tabs_real[1], tabs_real[2], tabs_real[0], tabs_real[1], tabs_real[2],
      tabs_meta[0], tabs_meta[1], tabs_meta[2])


def _softplus(z):
    return jnp.maximum(z, 0.0) + jnp.log1p(jnp.exp(-jnp.abs(z)))


def _lru_kernel(xr_ref, prev_ref, hist0_ref, xg_ref, cw_ref, cb_ref, wax_ref, ba_ref, bx_ref, lam_ref, h0_ref,
                y_ref, hl_ref, a_s, u_s, hcar, *, tc):
    c = pl.program_id(1)

    @pl.when(c == 0)
    def _():
        hcar[...] = jnp.broadcast_to(h0_ref[...], hcar.shape)

    first = c == 0
    for nb in range(LRU_BLOCKS):
        cs = slice(nb * LRU_BW, (nb + 1) * LRU_BW)
        hist = jnp.where(first, hist0_ref[:, cs], prev_ref[:, cs]).astype(F32)
        x = xr_ref[:, cs].astype(F32)
        ext = jnp.concatenate([hist, x], axis=0)
        cw = cw_ref[:, cs]
        y = cb_ref[:, cs] + x * cw[CONV_W - 1:CONV_W]
        for d in range(1, CONV_W):
            y = y + pltpu.roll(ext, d, 0)[N_META:] * cw[CONV_W - 1 - d:CONV_W - d]
        gates = jnp.dot(y.astype(BF16), wax_ref[nb], preferred_element_type=F32)
        r = jax.nn.sigmoid(gates[:, :LRU_BW] + ba_ref[:, cs])
        ig = jax.nn.sigmoid(gates[:, LRU_BW:] + bx_ref[:, cs])
        log_a = (-RGLRU_C) * r * _softplus(-lam_ref[:, cs])
        a = jnp.exp(log_a)
        a_s[:, cs] = a
        u_s[:, cs] = jnp.sqrt(-jnp.tanh(log_a) * (a * a + 1.0)) * (ig * y)

    row = lax.broadcasted_iota(I32, (8, D_LRU), 0)

    def body(i, h):
        sl = pl.ds(pl.multiple_of(i * 8, 8), 8)
        a = a_s[sl, :]
        u = u_s[sl, :]
        for d in (1, 2, 4):
            ok = row >= d
            u = jnp.where(ok, a * pltpu.roll(u, d, 0) + u, u)
            a = jnp.where(ok, a * pltpu.roll(a, d, 0), a)
        hs = a * h + u
        u_s[sl, :] = hs
        return jnp.broadcast_to(hs[7:8, :], (8, D_LRU))

    h_fin = lax.fori_loop(0, tc // 8, body, hcar[...])
    hcar[...] = h_fin
    hl_ref[0] = h_fin
    for nb in range(LRU_BLOCKS):
        cs = slice(nb * LRU_BW, (nb + 1) * LRU_BW)
        y_ref[:, cs] = (u_s[:, cs] * jax.nn.gelu(xg_ref[:, cs].astype(F32))).astype(BF16)


def _lru(proj, hist0, h0, cw, cb, wax, ba, bx, lam, batch, seq, tc):
    nch = seq // tc
    per16 = tc // N_META

    def prev_idx(b, c):
        return (jnp.maximum(b * (seq // N_META) + c * per16 - 1, 0), 1)

    vec = pl.BlockSpec((1, D_LRU), lambda b, c: (0, 0))
    in_specs = [
        pl.BlockSpec((tc, D_LRU), lambda b, c: (b * nch + c, 1)),
        pl.BlockSpec((N_META, D_LRU), prev_idx),
        pl.BlockSpec((N_META, D_LRU), lambda b, c: (0, 0)),
        pl.BlockSpec((tc, D_LRU), lambda b, c: (b * nch + c, 2)),
        pl.BlockSpec((CONV_W, D_LRU), lambda b, c: (0, 0)),
        vec,
        pl.BlockSpec((LRU_BLOCKS, LRU_BW, 2 * LRU_BW), lambda b, c: (0, 0, 0)),
        vec, vec, vec, vec,
    ]
    return pl.pallas_call(
        functools.partial(_lru_kernel, tc=tc),
        out_shape=(jax.ShapeDtypeStruct((batch * seq, D_LRU), BF16),
                   jax.ShapeDtypeStruct((batch, 8, D_LRU), F32)),
        grid=(batch, nch),
        in_specs=in_specs,
        out_specs=(pl.BlockSpec((tc, D_LRU), lambda b, c: (b * nch + c, 0)),
                   pl.BlockSpec((1, 8, D_LRU), lambda b, c: (b, 0, 0))),
        scratch_shapes=[pltpu.VMEM((tc, D_LRU), F32), pltpu.VMEM((tc, D_LRU), F32), pltpu.VMEM((8, D_LRU), F32)],
        compiler_params=_cparams(("arbitrary", "arbitrary")),
        name="conv_rglru",
    )(proj, proj, hist0, proj, cw, cb, wax, ba, bx, lam, h0)


def _merge_kernel(o_ref, y_ref, wa_ref, wl_ref, ga_ref, gl_ref, z_ref):
    ya = jnp.dot(o_ref[...], wa_ref[...], preferred_element_type=F32)
    yl = jnp.dot(y_ref[...], wl_ref[...], preferred_element_type=F32)
    z = jax.nn.sigmoid(ga_ref[...].astype(F32)) * ya + jax.nn.sigmoid(gl_ref[...].astype(F32)) * yl
    z_ref[...] = z.astype(BF16)


def _merge(o_attn, y_lru, wa, wl, proj, tm=512, tn=1024):
    n_tok = o_attn.shape[0]
    ga_blk = 3 * D_MODEL // tn
    gl_blk = 4 * D_MODEL // tn
    return pl.pallas_call(
        _merge_kernel,
        out_shape=jax.ShapeDtypeStruct((n_tok, D_MODEL), BF16),
        grid=(D_MODEL // tn, n_tok // tm),
        in_specs=[pl.BlockSpec((tm, D_MODEL), lambda j, i: (i, 0)),
                  pl.BlockSpec((tm, D_MODEL), lambda j, i: (i, 0)),
                  pl.BlockSpec((D_MODEL, tn), lambda j, i: (0, j)),
                  pl.BlockSpec((D_MODEL, tn), lambda j, i: (0, j)),
                  pl.BlockSpec((tm, tn), lambda j, i: (i, ga_blk + j)),
                  pl.BlockSpec((tm, tn), lambda j, i: (i, gl_blk + j))],
        out_specs=pl.BlockSpec((tm, tn), lambda j, i: (i, j)),
        compiler_params=_cparams(("parallel", "parallel")),
        name="gated_merge",
    )(o_attn, y_lru, wa, wl, proj, proj)


def _layernorm(v, g, b):
    mu = jnp.mean(v, axis=-1, keepdims=True)
    dv = v - mu
    var = jnp.mean(dv * dv, axis=-1, keepdims=True)
    return dv * lax.rsqrt(var + LN_EPS) * g + b


def _pack_bf16_pairs(v):
    bits = lax.bitcast_convert_type(v.astype(BF16).astype(F32), U32)
    return (bits[:, :HALF_PACK] >> 16) | (bits[:, HALF_PACK:] & jnp.uint32(0xFFFF0000))


def _unpack_bf16_pairs(w):
    lo = lax.bitcast_convert_type(w << 16, F32)
    hi = lax.bitcast_convert_type(w & jnp.uint32(0xFFFF0000), F32)
    return lo, hi


def _ln1_kernel(x_ref, z_ref, w_ref, g_ref, b_ref, h_ref, hb_ref, hp_ref):
    mix = jnp.dot(z_ref[...], w_ref[...], preferred_element_type=F32)
    h = _layernorm(DN_ALPHA * x_ref[...] + mix, g_ref[...], b_ref[...])
    h_ref[...] = h
    hb_ref[...] = h.astype(BF16)
    hp_ref[...] = _pack_bf16_pairs(h)


def _ln1(x2, z, w_out, g, b, tm=256):
    n_tok = x2.shape[0]
    row = lambda i: (i, 0)
    fixed = lambda i: (0, 0)
    return pl.pallas_call(
        _ln1_kernel,
        out_shape=(jax.ShapeDtypeStruct((n_tok, D_MODEL), F32),
                   jax.ShapeDtypeStruct((n_tok, D_MODEL), BF16),
                   jax.ShapeDtypeStruct((n_tok, HALF_PACK), U32)),
        grid=(n_tok // tm,),
        in_specs=[pl.BlockSpec((tm, D_MODEL), row), pl.BlockSpec((tm, D_MODEL), row),
                  pl.BlockSpec((D_MODEL, D_MODEL), fixed),
                  pl.BlockSpec((1, D_MODEL), fixed), pl.BlockSpec((1, D_MODEL), fixed)],
        out_specs=(pl.BlockSpec((tm, D_MODEL), row), pl.BlockSpec((tm, D_MODEL), row),
                   pl.BlockSpec((tm, HALF_PACK), row)),
        compiler_params=_cparams(("parallel",)),
        name="outproj_ln1",
    )(x2, z, w_out, g, b)


def _router_kernel(h_ref, w_ref, bias_ref, ti_ref, tw_ref, *, tm):
    logits = lax.dot_general(w_ref[...], h_ref[...], (((1,), (1,)), ((), ())), preferred_element_type=F32)
    scores = jax.nn.sigmoid(logits)
    sel = scores + bias_ref[...]
    neg = -jnp.inf
    e_iota = lax.broadcasted_iota(I32, (N_EXPERTS, tm), 0).astype(F32)
    g_iota32 = lax.broadcasted_iota(I32, (GROUP_SIZE, tm), 0).astype(F32)
    gs_rows = []
    for g in range(N_GROUPS):
        blk = sel[g * GROUP_SIZE:(g + 1) * GROUP_SIZE]
        m1 = jnp.max(blk, axis=0, keepdims=True)
        i1 = jnp.min(jnp.where(blk == m1, g_iota32, float(GROUP_SIZE)), axis=0, keepdims=True)
        m2 = jnp.max(jnp.where(g_iota32 == i1, neg, blk), axis=0, keepdims=True)
        gs_rows.append(m1 + m2)
    gs = jnp.concatenate(gs_rows, axis=0)
    g_iota = lax.broadcasted_iota(I32, (N_GROUPS, tm), 0).astype(F32)
    gsel = jnp.zeros((N_GROUPS, tm), F32)
    for _ in range(TOPK_GROUPS):
        m = jnp.max(gs, axis=0, keepdims=True)
        idx = jnp.min(jnp.where(gs == m, g_iota, float(N_GROUPS)), axis=0, keepdims=True)
        hit = g_iota == idx
        gsel = jnp.where(hit, 1.0, gsel)
        gs = jnp.where(hit, neg, gs)
    masked = jnp.concatenate(
        [jnp.where(gsel[g:g + 1] > 0.0, sel[g * GROUP_SIZE:(g + 1) * GROUP_SIZE], neg) for g in range(N_GROUPS)],
        axis=0)
    tis, tws = [], []
    for _ in range(TOP_K):
        m = jnp.max(masked, axis=0, keepdims=True)
        idx = jnp.min(jnp.where(masked == m, e_iota, float(N_EXPERTS)), axis=0, keepdims=True)
        hit = e_iota == idx
        tis.append(idx)
        tws.append(jnp.sum(jnp.where(hit, scores, 0.0), axis=0, keepdims=True))
        masked = jnp.where(hit, neg, masked)
    tw = jnp.concatenate(tws, axis=0)
    ti_ref[...] = jnp.concatenate(tis, axis=0).astype(I32)
    tw_ref[...] = tw / jnp.sum(tw, axis=0, keepdims=True) * ROUTED_SCALE


def _router(hb, wr_t, bias, tm=512):
    n_tok = hb.shape[0]
    return pl.pallas_call(
        functools.partial(_router_kernel, tm=tm),
        out_shape=(jax.ShapeDtypeStruct((TOP_K, n_tok), I32), jax.ShapeDtypeStruct((TOP_K, n_tok), F32)),
        grid=(n_tok // tm,),
        in_specs=[pl.BlockSpec((tm, D_MODEL), lambda i: (i, 0)),
                  pl.BlockSpec((N_EXPERTS, D_MODEL), lambda i: (0, 0)),
                  pl.BlockSpec((N_EXPERTS, 1), lambda i: (0, 0))],
        out_specs=(pl.BlockSpec((TOP_K, tm), lambda i: (0, i)), pl.BlockSpec((TOP_K, tm), lambda i: (0, i))),
        compiler_params=_cparams(("parallel",)),
        name="router_topk",
    )(hb, wr_t, bias)


def _expert_kernel(blk_e_ref, nused_ref, tokc_ref, tokn_ref, hp_ref, wg_ref, wu_ref, wd_ref, ys_ref,
                   xbuf, sem, wgb, wub, wdb):
    i = pl.program_id(0)
    nu = nused_ref[0]

    def issue(tok_ref, slot):
        for r in range(EXPERT_BLOCK):
            t = tok_ref[0, 0, r]
            pltpu.make_async_copy(hp_ref.at[pl.ds(t, 1), :], xbuf.at[slot, pl.ds(r, 1), :], sem.at[slot]).start()

    @pl.when(i == 0)
    def _():
        issue(tokc_ref, 0)

    @pl.when(i + 1 < nu)
    def _():
        issue(tokn_ref, (i + 1) & 1)

    @pl.when(i < nu)
    def _():
        slot = i & 1
        pltpu.make_async_copy(hp_ref.at[pl.ds(0, EXPERT_BLOCK), :], xbuf.at[slot], sem.at[slot]).wait()
        new_expert = jnp.logical_or(i == 0, blk_e_ref[i] != blk_e_ref[jnp.maximum(i - 1, 0)])

        @pl.when(new_expert)
        def _():
            wgb[...] = wg_ref[0].astype(BF16)
            wub[...] = wu_ref[0].astype(BF16)
            wdb[...] = wd_ref[0].astype(BF16)

        lo, hi = _unpack_bf16_pairs(xbuf[slot])
        x = jnp.concatenate([lo.astype(BF16), hi.astype(BF16)], axis=1)
        hid = jax.nn.silu(jnp.dot(x, wgb[...], preferred_element_type=F32)) * jnp.dot(
            x, wub[...], preferred_element_type=F32)
        out = jnp.dot(hid.astype(BF16), wdb[...], preferred_element_type=F32)
        ys_ref[...] = _pack_bf16_pairs(out)

    @pl.when(i >= nu)
    def _():
        ys_ref[...] = jnp.zeros(ys_ref.shape, U32)


def _experts(blk_e, nused, row_tok3, hp_pad, wg, wu, wd):
    n_blocks = row_tok3.shape[0]
    tok_spec = pl.BlockSpec((1, 1, EXPERT_BLOCK), lambda i, be, nu: (i, 0, 0), memory_space=pltpu.SMEM)
    tokn_spec = pl.BlockSpec((1, 1, EXPERT_BLOCK), lambda i, be, nu: (jnp.minimum(i + 1, n_blocks - 1), 0, 0),
                             memory_space=pltpu.SMEM)
    grid_spec = pltpu.PrefetchScalarGridSpec(
        num_scalar_prefetch=2, grid=(n_blocks,),
        in_specs=[tok_spec, tokn_spec, pl.BlockSpec(memory_space=pl.ANY),
                  pl.BlockSpec((1, D_MODEL, D_EXPERT), lambda i, be, nu: (be[i], 0, 0)),
                  pl.BlockSpec((1, D_MODEL, D_EXPERT), lambda i, be, nu: (be[i], 0, 0)),
                  pl.BlockSpec((1, D_EXPERT, D_MODEL), lambda i, be, nu: (be[i], 0, 0))],
        out_specs=pl.BlockSpec((EXPERT_BLOCK, HALF_PACK), lambda i, be, nu: (i, 0)),
        scratch_shapes=[pltpu.VMEM((2, EXPERT_BLOCK, HALF_PACK), U32), pltpu.SemaphoreType.DMA((2,)),
                        pltpu.VMEM((D_MODEL, D_EXPERT), BF16), pltpu.VMEM((D_MODEL, D_EXPERT), BF16),
                        pltpu.VMEM((D_EXPERT, D_MODEL), BF16)])
    return pl.pallas_call(
        _expert_kernel,
        out_shape=jax.ShapeDtypeStruct((n_blocks * EXPERT_BLOCK, HALF_PACK), U32),
        grid_spec=grid_spec,
        compiler_params=_cparams(("arbitrary",)),
        name="routed_experts",
    )(blk_e, nused, row_tok3, row_tok3, hp_pad, wg, wu, wd)


def _combine_kernel(dc_ref, dn_ref, ys_ref, tw_ref, h_ref, hb_ref, wg_ref, wu_ref, wd_ref, g_ref, b_ref, o_ref,
                    gbuf, sem, *, tm):
    i = pl.program_id(0)
    nsteps = pl.num_programs(0)

    def issue(d_ref, slot):
        def body(t, carry):
            for k in range(TOP_K):
                pltpu.make_async_copy(ys_ref.at[pl.ds(d_ref[0, k, t], 1), :],
                                      gbuf.at[slot, k, pl.ds(t, 1), :], sem.at[slot]).start()
            return carry
        lax.fori_loop(0, tm, body, 0)

    @pl.when(i == 0)
    def _():
        issue(dc_ref, 0)

    @pl.when(i + 1 < nsteps)
    def _():
        issue(dn_ref, (i + 1) & 1)

    slot = i & 1
    hb = hb_ref[...]
    hid = jax.nn.silu(jnp.dot(hb, wg_ref[...], preferred_element_type=F32)) * jnp.dot(
        hb, wu_ref[...], preferred_element_type=F32)
    shared = jnp.dot(hid.astype(BF16), wd_ref[...], preferred_element_type=F32)
    for k in range(TOP_K):
        pltpu.make_async_copy(ys_ref.at[pl.ds(0, tm), :], gbuf.at[slot, k], sem.at[slot]).wait()
    tw = tw_ref[...]
    acc_lo = jnp.zeros((tm, HALF_PACK), F32)
    acc_hi = jnp.zeros((tm, HALF_PACK), F32)
    for k in range(TOP_K):
        lo, hi = _unpack_bf16_pairs(gbuf[slot, k])
        wk = tw[:, k:k + 1]
        acc_lo = acc_lo + wk * lo
        acc_hi = acc_hi + wk * hi
    routed = jnp.concatenate([acc_lo, acc_hi], axis=1)
    v = DN_ALPHA * h_ref[...] + (routed + shared)
    o_ref[...] = _layernorm(v, g_ref[...], b_ref[...])


def _combine(dest3, ys, topw, h1, h1b, wgs, wus, wds, g, b, tm=256):
    n_tok = h1.shape[0]
    nsteps = n_tok // tm
    row = lambda i: (i, 0)
    fixed = lambda i: (0, 0)
    return pl.pallas_call(
        functools.partial(_combine_kernel, tm=tm),
        out_shape=jax.ShapeDtypeStruct((n_tok, D_MODEL), F32),
        grid=(nsteps,),
        in_specs=[pl.BlockSpec((1, TOP_K, tm), lambda i: (i, 0, 0), memory_space=pltpu.SMEM),
                  pl.BlockSpec((1, TOP_K, tm), lambda i: (jnp.minimum(i + 1, nsteps - 1), 0, 0),
                               memory_space=pltpu.SMEM),
                  pl.BlockSpec(memory_space=pl.ANY),
                  pl.BlockSpec((tm, TOP_K), row),
                  pl.BlockSpec((tm, D_MODEL), row), pl.BlockSpec((tm, D_MODEL), row),
                  pl.BlockSpec((D_MODEL, D_EXPERT), fixed), pl.BlockSpec((D_MODEL, D_EXPERT), fixed),
                  pl.BlockSpec((D_EXPERT, D_MODEL), fixed),
                  pl.BlockSpec((1, D_MODEL), fixed), pl.BlockSpec((1, D_MODEL), fixed)],
        out_specs=pl.BlockSpec((tm, D_MODEL), row),
        scratch_shapes=[pltpu.VMEM((2, TOP_K, tm, HALF_PACK), U32), pltpu.SemaphoreType.DMA((2,))],
        compiler_params=_cparams(("arbitrary",)),
        name="combine_shared_ln2",
    )(dest3, dest3, ys, topw, h1, h1b, wgs, wus, wds, g, b)


def _dispatch_plan(topi, n_tok):
    n_assign = n_tok * TOP_K
    n_blocks = -(-(n_assign + N_EXPERTS * (EXPERT_BLOCK - 1)) // EXPERT_BLOCK)
    flat_e = topi.reshape(-1)
    order = jnp.argsort(flat_e)
    e_sorted = flat_e[order]
    tok_sorted = (order // TOP_K).astype(I32)
    counts = jnp.bincount(flat_e, length=N_EXPERTS)
    padded = (counts + EXPERT_BLOCK - 1) // EXPERT_BLOCK * EXPERT_BLOCK
    grp_start = jnp.cumsum(counts) - counts
    pad_end = jnp.cumsum(padded)
    pad_start = pad_end - padded
    dest_sorted = (pad_start[e_sorted] + (jnp.arange(n_assign) - grp_start[e_sorted])).astype(I32)
    row_tok = jnp.full((n_blocks * EXPERT_BLOCK,), n_tok, I32).at[dest_sorted].set(tok_sorted)
    dest = jnp.zeros((n_assign,), I32).at[order].set(dest_sorted)
    nused = (pad_end[-1] // EXPERT_BLOCK).astype(I32)
    blk = jnp.arange(n_blocks, dtype=I32)
    blk_e = jnp.minimum(jnp.searchsorted(pad_end, blk * EXPERT_BLOCK, side='right'), N_EXPERTS - 1).astype(I32)
    blk_e = jnp.where(blk < nused, blk_e, blk_e[jnp.maximum(nused - 1, 0)])
    return blk_e, nused.reshape(1), row_tok.reshape(n_blocks, 1, EXPERT_BLOCK), dest.reshape(n_tok, TOP_K)


def _rope_tables(n_pos):
    half = ROT_DIM // 2
    inv_freq = ROPE_THETA ** (-jnp.arange(half, dtype=F32) / half)
    ang = jnp.arange(n_pos).astype(F32)[:, None] * inv_freq[None, :]
    cos, sin = jnp.cos(ang), jnp.sin(ang)
    c = jnp.concatenate([cos, cos, jnp.ones((n_pos, HEAD_DIM - ROT_DIM), F32)], axis=1)
    s1 = jnp.concatenate([-sin, jnp.zeros((n_pos, HEAD_DIM - half), F32)], axis=1)
    s2 = jnp.concatenate([jnp.zeros((n_pos, half), F32), sin, jnp.zeros((n_pos, HEAD_DIM - ROT_DIM), F32)], axis=1)
    return c, s1, s2


def kernel(x, meta_tokens, w_in, conv_w, conv_b, w_rg_a, b_rg_a, w_rg_x, b_rg_x, rg_lambda, attn_sinks, w_o_attn, w_o_lru, w_out, ln1_g, ln1_b, w_router, router_bias, w_gate_e, w_up_e, w_down_e, w_gate_s, w_up_s, w_down_s, ln2_g, ln2_b):
    batch, seq, _ = x.shape
    n_tok = batch * seq
    assert w_in.shape[0] == 1 and seq % 512 == 0 and meta_tokens.shape[0] == N_META
    x2 = x.reshape(n_tok, D_MODEL)
    row = lambda v: v.reshape(1, -1).astype(F32)

    q_end, k_end, v_end = N_Q_HEADS * HEAD_DIM, N_Q_HEADS * HEAD_DIM + KV_DIM, N_Q_HEADS * HEAD_DIM + 2 * KV_DIM
    wi = w_in[0]
    w_in_p = jnp.concatenate([wi[:, :q_end], wi[:, v_end:], wi[:, q_end:k_end], wi[:, k_end:v_end]], axis=1).astype(BF16)
    wax = jnp.concatenate([w_rg_a[0], w_rg_x[0]], axis=-1).astype(BF16)

    c_t, s1_t, s2_t = _rope_tables(N_META + seq)
    tabs_meta = (c_t[:N_META], s1_t[:N_META], s2_t[:N_META])
    tabs_real = (c_t[N_META:], s1_t[N_META:], s2_t[N_META:])

    projm = _matmul(meta_tokens.astype(BF16), w_in_p, BF16, N_META, 1024)
    lru_args = (conv_w[0], row(conv_b[0]), wax, row(b_rg_a[0]), row(b_rg_x[0]), row(rg_lambda[0]))
    _, h_meta = _lru(projm, jnp.zeros((N_META, D_LRU), BF16), jnp.zeros((1, D_LRU), F32), *lru_args,
                     batch=1, seq=N_META, tc=N_META)

    proj = _matmul(x2.astype(BF16), w_in_p, BF16, 1024, 1024)
    o_attn = _attention(proj, projm, attn_sinks[0].astype(F32), tabs_real, tabs_meta, batch, seq)
    y_lru, _ = _lru(proj, projm[:, D_MODEL:2 * D_MODEL], h_meta[0, 7:8], *lru_args, batch=batch, seq=seq, tc=256)
    z = _merge(o_attn, y_lru, w_o_attn[0].astype(BF16), w_o_lru[0].astype(BF16), proj)
    h1, h1b, h1p = _ln1(x2, z, w_out[0].astype(BF16), row(ln1_g[0]), row(ln1_b[0]))

    topi_t, topw_t = _router(h1b, w_router[0].T.astype(BF16), router_bias[0].reshape(N_EXPERTS, 1).astype(F32))
    blk_e, nused, row_tok3, dest = _dispatch_plan(topi_t.T, n_tok)
    hp_pad = jnp.concatenate([h1p, jnp.zeros((1, HALF_PACK), U32)], axis=0)
    ys = _experts(blk_e, nused, row_tok3, hp_pad, w_gate_e[0], w_up_e[0], w_down_e[0])
    tm_c = 256
    dest3 = dest.reshape(n_tok // tm_c, tm_c, TOP_K).transpose(0, 2, 1)
    out = _combine(dest3, ys, topw_t.T, h1, h1b, w_gate_s[0].astype(BF16), w_up_s[0].astype(BF16),
                   w_down_s[0].astype(BF16), row(ln2_g[0]), row(ln2_b[0]), tm=tm_c)
    return out.reshape(batch, seq, D_MODEL)
```

```python
import functools

import jax
import jax.numpy as jnp
import numpy as np
from jax import lax
from jax.experimental import pallas as pl
from jax.experimental.pallas import tpu as pltpu

F32 = jnp.float32
BF16 = jnp.bfloat16
U32 = jnp.uint32
I32 = jnp.int32

D_MODEL = 2048
N_META = 16
N_Q_HEADS = 16
N_KV_HEADS = 4
HEAD_DIM = 128
Q_GROUP = N_Q_HEADS // N_KV_HEADS
ROT_DIM = HEAD_DIM // 4
ROPE_THETA = 500000.0
ATTN_BLOCK = 128
D_LRU = D_MODEL
LRU_BLOCKS = 16
LRU_BW = D_LRU // LRU_BLOCKS
CONV_W = 4
RGLRU_C = 8.0
N_EXPERTS = 256
TOP_K = 8
N_GROUPS = 8
GROUP_SIZE = N_EXPERTS // N_GROUPS
TOPK_GROUPS = 4
D_EXPERT = 512
ROUTED_SCALE = 2.5
EXPERT_BLOCK = 128
DN_ALPHA = 2.0 ** 0.25
LN_EPS = 1e-5
NEG_INF = -1e30
KV_DIM = N_KV_HEADS * HEAD_DIM
IN_DIM = 5 * D_MODEL + 2 * KV_DIM
K_COLBLK = 5 * D_MODEL // KV_DIM
V_COLBLK = K_COLBLK + 1
VMEM_LIMIT = 50 * 1024 * 1024


def _cparams(sem):
    return pltpu.CompilerParams(dimension_semantics=sem, vmem_limit_bytes=VMEM_LIMIT)


def _mm_kernel(a_ref, b_ref, o_ref):
    o_ref[...] = jnp.dot(a_ref[...], b_ref[...], preferred_element_type=F32).astype(o_ref.dtype)


def _matmul(a, b, out_dtype, tm, tn):
    m, k = a.shape
    n = b.shape[1]
    return pl.pallas_call(
        _mm_kernel,
        out_shape=jax.ShapeDtypeStruct((m, n), out_dtype),
        grid=(n // tn, m // tm),
        in_specs=[pl.BlockSpec((tm, k), lambda j, i: (i, 0)),
                  pl.BlockSpec((k, tn), lambda j, i: (0, j))],
        out_specs=pl.BlockSpec((tm, tn), lambda j, i: (i, j)),
        compiler_params=_cparams(("parallel", "parallel")),
        name="dense_matmul",
    )(a, b)


def _rope(x, c, s1, s2):
    return x * c + pltpu.roll(x, HEAD_DIM - ROT_DIM // 2, 1) * s1 + pltpu.roll(x, ROT_DIM // 2, 1) * s2


def _attn_kernel(sink_ref, q_ref, kc_ref, kp_ref, vc_ref, vp_ref, km_ref, vm_ref,
                 cq_ref, s1q_ref, s2q_ref, cp_ref, s1p_ref, s2p_ref, cm_ref, s1m_ref, s2m_ref,
                 o_ref, *, tq):
    n = pl.program_id(1)
    nsub = tq // ATTN_BLOCK
    scale = HEAD_DIM ** -0.5
    cq, s1q, s2q = cq_ref[...], s1q_ref[...], s2q_ref[...]
    cp, s1p, s2p = cp_ref[...], s1p_ref[...], s2p_ref[...]
    cm, s1m, s2m = cm_ref[...], s1m_ref[...], s2m_ref[...]
    rows = Q_GROUP * ATTN_BLOCK
    r_idx = lax.broadcasted_iota(I32, (rows, ATTN_BLOCK), 0) % ATTN_BLOCK
    c_idx = lax.broadcasted_iota(I32, (rows, ATTN_BLOCK), 1)
    cur_ok = c_idx <= r_idx
    prev_ok = c_idx > r_idx
    dn = (((1,), (1,)), ((), ()))
    for g in range(N_KV_HEADS):
        hs = slice(g * HEAD_DIM, (g + 1) * HEAD_DIM)
        k_g = _rope(kc_ref[:, hs].astype(F32), cq, s1q, s2q).astype(BF16)
        kp_g = _rope(kp_ref[:, hs].astype(F32), cp, s1p, s2p).astype(BF16)
        km_g = _rope(km_ref[:, hs].astype(F32), cm, s1m, s2m).astype(BF16)
        v_g = vc_ref[:, hs]
        vp_g = vp_ref[:, hs]
        vm_g = vm_ref[:, hs]
        sink = jnp.concatenate(
            [jnp.full((ATTN_BLOCK, 1), sink_ref[g * Q_GROUP + j], F32) for j in range(Q_GROUP)], axis=0)
        for sub in range(nsub):
            rs = slice(sub * ATTN_BLOCK, (sub + 1) * ATTN_BLOCK)
            q_st = jnp.concatenate(
                [_rope(q_ref[rs, (g * Q_GROUP + j) * HEAD_DIM:(g * Q_GROUP + j + 1) * HEAD_DIM].astype(F32),
                       cq[rs], s1q[rs], s2q[rs]) for j in range(Q_GROUP)], axis=0).astype(BF16)
            if sub == 0:
                k_prev, v_prev = kp_g, vp_g
                p_ok = (c_idx - r_idx) > jnp.where(n > 0, 0, ATTN_BLOCK)
            else:
                ps = slice((sub - 1) * ATTN_BLOCK, sub * ATTN_BLOCK)
                k_prev, v_prev = k_g[ps], v_g[ps]
                p_ok = prev_ok
            s_p = lax.dot_general(q_st, k_prev, dn, preferred_element_type=F32) * scale
            s_c = lax.dot_general(q_st, k_g[rs], dn, preferred_element_type=F32) * scale
            s_m = lax.dot_general(q_st, km_g, dn, preferred_element_type=F32) * scale
            s_p = jnp.where(p_ok, s_p, NEG_INF)
            s_c = jnp.where(cur_ok, s_c, NEG_INF)
            m = jnp.maximum(jnp.maximum(jnp.max(s_p, axis=1, keepdims=True), jnp.max(s_c, axis=1, keepdims=True)),
                            jnp.maximum(jnp.max(s_m, axis=1, keepdims=True), sink))
            p_p = jnp.exp(s_p - m)
            p_c = jnp.exp(s_c - m)
            p_m = jnp.exp(s_m - m)
            den = (jnp.sum(p_p, axis=1, keepdims=True) + jnp.sum(p_c, axis=1, keepdims=True)
                   + jnp.sum(p_m, axis=1, keepdims=True) + jnp.exp(sink - m))
            o = (jnp.dot(p_p.astype(BF16), v_prev, preferred_element_type=F32)
                 + jnp.dot(p_c.astype(BF16), v_g[rs], preferred_element_type=F32)
                 + jnp.dot(p_m.astype(BF16), vm_g, preferred_element_type=F32)) / den
            for j in range(Q_GROUP):
                h = g * Q_GROUP + j
                o_ref[rs, h * HEAD_DIM:(h + 1) * HEAD_DIM] = o[j * ATTN_BLOCK:(j + 1) * ATTN_BLOCK].astype(BF16)


def _attention(proj, projm, sinks, tabs_real, tabs_meta, batch, seq, tq=512):
    n_tok = batch * seq
    nq = seq // tq
    sub_per = tq // ATTN_BLOCK

    def prev_blk(b, n):
        return jnp.maximum(b * (seq // ATTN_BLOCK) + n * sub_per - 1, 0)

    tab_spec = pl.BlockSpec((tq, HEAD_DIM), lambda b, n, s: (n, 0))
    tabp_spec = pl.BlockSpec((ATTN_BLOCK, HEAD_DIM), lambda b, n, s: (jnp.maximum(n * sub_per - 1, 0), 0))
    tabm_spec = pl.BlockSpec((N_META, HEAD_DIM), lambda b, n, s: (0, 0))
    in_specs = [
        pl.BlockSpec((tq, D_MODEL), lambda b, n, s: (b * nq + n, 0)),
        pl.BlockSpec((tq, KV_DIM), lambda b, n, s: (b * nq + n, K_COLBLK)),
        pl.BlockSpec((ATTN_BLOCK, KV_DIM), lambda b, n, s: (prev_blk(b, n), K_COLBLK)),
        pl.BlockSpec((tq, KV_DIM), lambda b, n, s: (b * nq + n, V_COLBLK)),
        pl.BlockSpec((ATTN_BLOCK, KV_DIM), lambda b, n, s: (prev_blk(b, n), V_COLBLK)),
        pl.BlockSpec((N_META, KV_DIM), lambda b, n, s: (0, K_COLBLK)),
        pl.BlockSpec((N_META, KV_DIM), lambda b, n, s: (0, V_COLBLK)),
        tab_spec, tab_spec, tab_spec, tabp_spec, tabp_spec, tabp_spec, tabm_spec, tabm_spec, tabm_spec,
    ]
    grid_spec = pltpu.PrefetchScalarGridSpec(
        num_scalar_prefetch=1, grid=(batch, nq), in_specs=in_specs,
        out_specs=pl.BlockSpec((tq, D_MODEL), lambda b, n, s: (b * nq + n, 0)))
    return pl.pallas_call(
        functools.partial(_attn_kernel, tq=tq),
        out_shape=jax.ShapeDtypeStruct((n_tok, D_MODEL), BF16),
        grid_spec=grid_spec,
        compiler_params=_cparams(("parallel", "parallel")),
        name="swa_attention",
    )(sinks, proj, proj, proj, proj, proj, projm, projm,
      tabs_real[0], tabs_real[1], tabs_real[2], tabs_real[0], tabs_real[1], tabs_real[2],
      tabs_meta[0], tabs_meta[1], tabs_meta[2])


def _softplus(z):
    return jnp.maximum(z, 0.0) + jnp.log1p(jnp.exp(-jnp.abs(z)))


def _lru_kernel(xr_ref, prev_ref, hist0_ref, xg_ref, cw_ref, cb_ref, wax_ref, ba_ref, bx_ref, lam_ref, h0_ref,
                y_ref, hl_ref, a_s, u_s, hcar, *, tc):
    c = pl.program_id(1)

    @pl.when(c == 0)
    def _():
        hcar[...] = jnp.broadcast_to(h0_ref[...], hcar.shape)

    first = c == 0
    for nb in range(LRU_BLOCKS):
        cs = slice(nb * LRU_BW, (nb + 1) * LRU_BW)
        hist = jnp.where(first, hist0_ref[:, cs], prev_ref[:, cs]).astype(F32)
        x = xr_ref[:, cs].astype(F32)
        ext = jnp.concatenate([hist, x], axis=0)
        cw = cw_ref[:, cs]
        y = cb_ref[:, cs] + x * cw[CONV_W - 1:CONV_W]
        for d in range(1, CONV_W):
            y = y + pltpu.roll(ext, d, 0)[N_META:] * cw[CONV_W - 1 - d:CONV_W - d]
        gates = jnp.dot(y.astype(BF16), wax_ref[nb], preferred_element_type=F32)
        r = jax.nn.sigmoid(gates[:, :LRU_BW] + ba_ref[:, cs])
        ig = jax.nn.sigmoid(gates[:, LRU_BW:] + bx_ref[:, cs])
        log_a = (-RGLRU_C) * r * _softplus(-lam_ref[:, cs])
        a = jnp.exp(log_a)
        a_s[:, cs] = a
        u_s[:, cs] = jnp.sqrt(-jnp.tanh(log_a) * (a * a + 1.0)) * (ig * y)

    row = lax.broadcasted_iota(I32, (8, D_LRU), 0)

    def body(i, h):
        sl = pl.ds(pl.multiple_of(i * 8, 8), 8)
        a = a_s[sl, :]
        u = u_s[sl, :]
        for d in (1, 2, 4):
            ok = row >= d
            u = jnp.where(ok, a * pltpu.roll(u, d, 0) + u, u)
            a = jnp.where(ok, a * pltpu.roll(a, d, 0), a)
        hs = a * h + u
        u_s[sl, :] = hs
        return jnp.broadcast_to(hs[7:8, :], (8, D_LRU))

    h_fin = lax.fori_loop(0, tc // 8, body, hcar[...])
    hcar[...] = h_fin
    hl_ref[0] = h_fin
    for nb in range(LRU_BLOCKS):
        cs = slice(nb * LRU_BW, (nb + 1) * LRU_BW)
        y_ref[:, cs] = (u_s[:, cs] * jax.nn.gelu(xg_ref[:, cs].astype(F32))).astype(BF16)


def _lru(proj, hist0, h0, cw, cb, wax, ba, bx, lam, batch, seq, tc):
    nch = seq // tc
    per16 = tc // N_META

    def prev_idx(b, c):
        return (jnp.maximum(b * (seq // N_META) + c * per16 - 1, 0), 1)

    vec = pl.BlockSpec((1, D_LRU), lambda b, c: (0, 0))
    in_specs = [
        pl.BlockSpec((tc, D_LRU), lambda b, c: (b * nch + c, 1)),
        pl.BlockSpec((N_META, D_LRU), prev_idx),
        pl.BlockSpec((N_META, D_LRU), lambda b, c: (0, 0)),
        pl.BlockSpec((tc, D_LRU), lambda b, c: (b * nch + c, 2)),
        pl.BlockSpec((CONV_W, D_LRU), lambda b, c: (0, 0)),
        vec,
        pl.BlockSpec((LRU_BLOCKS, LRU_BW, 2 * LRU_BW), lambda b, c: (0, 0, 0)),
        vec, vec, vec, vec,
    ]
    return pl.pallas_call(
        functools.partial(_lru_kernel, tc=tc),
        out_shape=(jax.ShapeDtypeStruct((batch * seq, D_LRU), BF16),
                   jax.ShapeDtypeStruct((batch, 8, D_LRU), F32)),
        grid=(batch, nch),
        in_specs=in_specs,
        out_specs=(pl.BlockSpec((tc, D_LRU), lambda b, c: (b * nch + c, 0)),
                   pl.BlockSpec((1, 8, D_LRU), lambda b, c: (b, 0, 0))),
        scratch_shapes=[pltpu.VMEM((tc, D_LRU), F32), pltpu.VMEM((tc, D_LRU), F32), pltpu.VMEM((8, D_LRU), F32)],
        compiler_params=_cparams(("arbitrary", "arbitrary")),
        name="conv_rglru",
    )(proj, proj, hist0, proj, cw, cb, wax, ba, bx, lam, h0)


def _merge_kernel(o_ref, y_ref, wa_ref, wl_ref, ga_ref, gl_ref, z_ref):
    ya = jnp.dot(o_ref[...], wa_ref[...], preferred_element_type=F32)
    yl = jnp.dot(y_ref[...], wl_ref[...], preferred_element_type=F32)
    z = jax.nn.sigmoid(ga_ref[...].astype(F32)) * ya + jax.nn.sigmoid(gl_ref[...].astype(F32)) * yl
    z_ref[...] = z.astype(BF16)


def _merge(o_attn, y_lru, wa, wl, proj, tm=512, tn=1024):
    n_tok = o_attn.shape[0]
    ga_blk = 3 * D_MODEL // tn
    gl_blk = 4 * D_MODEL // tn
    return pl.pallas_call(
        _merge_kernel,
        out_shape=jax.ShapeDtypeStruct((n_tok, D_MODEL), BF16),
        grid=(D_MODEL // tn, n_tok // tm),
        in_specs=[pl.BlockSpec((tm, D_MODEL), lambda j, i: (i, 0)),
                  pl.BlockSpec((tm, D_MODEL), lambda j, i: (i, 0)),
                  pl.BlockSpec((D_MODEL, tn), lambda j, i: (0, j)),
                  pl.BlockSpec((D_MODEL, tn), lambda j, i: (0, j)),
                  pl.BlockSpec((tm, tn), lambda j, i: (i, ga_blk + j)),
                  pl.BlockSpec((tm, tn), lambda j, i: (i, gl_blk + j))],
        out_specs=pl.BlockSpec((tm, tn), lambda j, i: (i, j)),
        compiler_params=_cparams(("parallel", "parallel")),
        name="gated_merge",
    )(o_attn, y_lru, wa, wl, proj, proj)


def _layernorm(v, g, b):
    mu = jnp.mean(v, axis=-1, keepdims=True)
    dv = v - mu
    var = jnp.mean(dv * dv, axis=-1, keepdims=True)
    return dv * lax.rsqrt(var + LN_EPS) * g + b


def _ln1_kernel(x_ref, z_ref, w_ref, g_ref, b_ref, h_ref, hb_ref):
    mix = jnp.dot(z_ref[...], w_ref[...], preferred_element_type=F32)
    h = _layernorm(DN_ALPHA * x_ref[...] + mix, g_ref[...], b_ref[...])
    h_ref[...] = h
    hb_ref[...] = h.astype(BF16)


def _ln1(x2, z, w_out, g, b, tm=256):
    n_tok = x2.shape[0]
    row = lambda i: (i, 0)
    fixed = lambda i: (0, 0)
    return pl.pallas_call(
        _ln1_kernel,
        out_shape=(jax.ShapeDtypeStruct((n_tok, D_MODEL), F32),
                   jax.ShapeDtypeStruct((n_tok, D_MODEL), BF16)),
        grid=(n_tok // tm,),
        in_specs=[pl.BlockSpec((tm, D_MODEL), row), pl.BlockSpec((tm, D_MODEL), row),
                  pl.BlockSpec((D_MODEL, D_MODEL), fixed),
                  pl.BlockSpec((1, D_MODEL), fixed), pl.BlockSpec((1, D_MODEL), fixed)],
        out_specs=(pl.BlockSpec((tm, D_MODEL), row), pl.BlockSpec((tm, D_MODEL), row)),
        compiler_params=_cparams(("parallel",)),
        name="outproj_ln1",
    )(x2, z, w_out, g, b)


def _router_kernel(h_ref, w_ref, bias_ref, ti_ref, tw_ref, *, tm):
    logits = lax.dot_general(w_ref[...], h_ref[...], (((1,), (1,)), ((), ())), preferred_element_type=F32)
    scores = jax.nn.sigmoid(logits)
    sel = scores + bias_ref[...]
    neg = -jnp.inf
    e_iota = lax.broadcasted_iota(I32, (N_EXPERTS, tm), 0).astype(F32)
    g_iota32 = lax.broadcasted_iota(I32, (GROUP_SIZE, tm), 0).astype(F32)
    gs_rows = []
    for g in range(N_GROUPS):
        blk = sel[g * GROUP_SIZE:(g + 1) * GROUP_SIZE]
        m1 = jnp.max(blk, axis=0, keepdims=True)
        i1 = jnp.min(jnp.where(blk == m1, g_iota32, float(GROUP_SIZE)), axis=0, keepdims=True)
        m2 = jnp.max(jnp.where(g_iota32 == i1, neg, blk), axis=0, keepdims=True)
        gs_rows.append(m1 + m2)
    gs = jnp.concatenate(gs_rows, axis=0)
    g_iota = lax.broadcasted_iota(I32, (N_GROUPS, tm), 0).astype(F32)
    gsel = jnp.zeros((N_GROUPS, tm), F32)
    for _ in range(TOPK_GROUPS):
        m = jnp.max(gs, axis=0, keepdims=True)
        idx = jnp.min(jnp.where(gs == m, g_iota, float(N_GROUPS)), axis=0, keepdims=True)
        hit = g_iota == idx
        gsel = jnp.where(hit, 1.0, gsel)
        gs = jnp.where(hit, neg, gs)
    masked = jnp.concatenate(
        [jnp.where(gsel[g:g + 1] > 0.0, sel[g * GROUP_SIZE:(g + 1) * GROUP_SIZE], neg) for g in range(N_GROUPS)],
        axis=0)
    tis, tws = [], []
    for _ in range(TOP_K):
        m = jnp.max(masked, axis=0, keepdims=True)
        idx = jnp.min(jnp.where(masked == m, e_iota, float(N_EXPERTS)), axis=0, keepdims=True)
        hit = e_iota == idx
        tis.append(idx)
        tws.append(jnp.sum(jnp.where(hit, scores, 0.0), axis=0, keepdims=True))
        masked = jnp.where(hit, neg, masked)
    tw = jnp.concatenate(tws, axis=0)
    ti_ref[...] = jnp.concatenate(tis, axis=0).astype(I32)
    tw_ref[...] = tw / jnp.sum(tw, axis=0, keepdims=True) * ROUTED_SCALE


def _router(hb, wr_t, bias, tm=512):
    n_tok = hb.shape[0]
    return pl.pallas_call(
        functools.partial(_router_kernel, tm=tm),
        out_shape=(jax.ShapeDtypeStruct((TOP_K, n_tok), I32), jax.ShapeDtypeStruct((TOP_K, n_tok), F32)),
        grid=(n_tok // tm,),
        in_specs=[pl.BlockSpec((tm, D_MODEL), lambda i: (i, 0)),
                  pl.BlockSpec((N_EXPERTS, D_MODEL), lambda i: (0, 0)),
                  pl.BlockSpec((N_EXPERTS, 1), lambda i: (0, 0))],
        out_specs=(pl.BlockSpec((TOP_K, tm), lambda i: (0, i)), pl.BlockSpec((TOP_K, tm), lambda i: (0, i))),
        compiler_params=_cparams(("parallel",)),
        name="router_topk",
    )(hb, wr_t, bias)


def _expert_kernel(nblk_ref, bst_ref, xs_ref, wg_ref, wu_ref, wd_ref, ys_ref,
                   xbuf, obuf, isem, osem, wgb, wub, wdb):
    e = pl.program_id(0)
    nb = nblk_ref[e]
    b0 = bst_ref[e]
    nused = bst_ref[N_EXPERTS - 1] + nblk_ref[N_EXPERTS - 1]

    def rows(g):
        return pl.ds(pl.multiple_of(g * EXPERT_BLOCK, EXPERT_BLOCK), EXPERT_BLOCK)

    def in_copy(g, slot):
        return pltpu.make_async_copy(xs_ref.at[rows(g), :], xbuf.at[slot], isem.at[slot])

    def out_copy(g, slot):
        return pltpu.make_async_copy(obuf.at[slot], ys_ref.at[rows(g), :], osem.at[slot])

    @pl.when(nb > 0)
    def _():
        @pl.when(b0 == 0)
        def _():
            in_copy(0, 0).start()

        wgb[...] = wg_ref[0].astype(BF16)
        wub[...] = wu_ref[0].astype(BF16)
        wdb[...] = wd_ref[0].astype(BF16)

        def body(j, carry):
            g = b0 + j
            slot = g & 1

            @pl.when(g + 1 < nused)
            def _():
                in_copy(g + 1, 1 - slot).start()

            in_copy(g, slot).wait()
            x = xbuf[slot].astype(BF16)
            hid =jax.nn.silu(jnp.dot(x, wgb[...], preferred_element_type=F32)) * jnp.dot(
                x, wub[...], preferred_element_type=F32)
            out = jnp.dot(hid.astype(BF16), wdb[...], preferred_element_type=F32)

            @pl.when(g >= 2)
            def _():
                out_copy(g - 2, slot).wait()

            obuf[slot] = out
            out_copy(g, slot).start()
            return carry

        lax.fori_loop(0, nb, body, 0)

    @pl.when(e == N_EXPERTS - 1)
    def _():
        @pl.when(nused >= 2)
        def _():
            out_copy(nused - 2, nused & 1).wait()

        out_copy(nused - 1, (nused - 1) & 1).wait()


def _experts(nblk, bst, xs, wg, wu, wd):
    grid_spec = pltpu.PrefetchScalarGridSpec(
        num_scalar_prefetch=2, grid=(N_EXPERTS,),
        in_specs=[pl.BlockSpec(memory_space=pl.ANY),
                  pl.BlockSpec((1, D_MODEL, D_EXPERT), lambda e, nb, bs: (e, 0, 0)),
                  pl.BlockSpec((1, D_MODEL, D_EXPERT), lambda e, nb, bs: (e, 0, 0)),
                  pl.BlockSpec((1, D_EXPERT, D_MODEL), lambda e, nb, bs: (e, 0, 0))],
        out_specs=pl.BlockSpec(memory_space=pl.ANY),
        scratch_shapes=[pltpu.VMEM((2, EXPERT_BLOCK, D_MODEL), F32), pltpu.VMEM((2, EXPERT_BLOCK, D_MODEL), F32),
                        pltpu.SemaphoreType.DMA((2,)), pltpu.SemaphoreType.DMA((2,)),
                        pltpu.VMEM((D_MODEL, D_EXPERT), BF16), pltpu.VMEM((D_MODEL, D_EXPERT), BF16),
                        pltpu.VMEM((D_EXPERT, D_MODEL), BF16)])
    return pl.pallas_call(
        _expert_kernel,
        out_shape=jax.ShapeDtypeStruct(xs.shape, F32),
        grid_spec=grid_spec,
        input_output_aliases={2: 0},
        compiler_params=_cparams(("arbitrary",)),
        name="routed_experts",
    )(nblk, bst, xs, wg, wu, wd)


COMBINE_CHUNK = 16


def _combine_kernel(dc_ref, dn_ref, ys_ref, tw_ref, h_ref, sh_ref, g_ref, b_ref, o_ref, gbuf, sem, *, tm):
    i = pl.program_id(0)
    nsteps = pl.num_programs(0)
    slot = i & 1
    nslot = 1 - slot
    nchunks = tm // COMBINE_CHUNK

    def issue_rows(d_ref, s, t0):
        for tt in range(COMBINE_CHUNK):
            for k in range(TOP_K):
                pltpu.make_async_copy(ys_ref.at[pl.ds(d_ref[0, k, t0 + tt], 1), :],
                                      gbuf.at[s, k, pl.ds(t0 + tt, 1), :], sem.at[s]).start()

    def wait_slabs(s):
        for k in range(TOP_K):
            pltpu.make_async_copy(ys_ref.at[pl.ds(0, tm), :], gbuf.at[s, k], sem.at[s]).wait()

    @pl.when(i == 0)
    def _():
        def prime(c, carry):
            issue_rows(dc_ref, 0, c * COMBINE_CHUNK)
            return carry
        lax.fori_loop(0, nchunks, prime, 0)

    wait_slabs(slot)
    g = g_ref[...]
    b = b_ref[...]

    def chunk(c, carry):
        r0 = pl.multiple_of(c * COMBINE_CHUNK, COMBINE_CHUNK)
        issue_rows(dn_ref, nslot, r0)
        rs = pl.ds(r0, COMBINE_CHUNK)
        tw = tw_ref[rs, :]
        routed = tw[:, 0:1] * gbuf[slot, 0, rs, :]
        for k in range(1, TOP_K):
            routed = routed + tw[:, k:k + 1] * gbuf[slot, k, rs, :]
        v =DN_ALPHA * h_ref[rs, :] + (routed + sh_ref[rs, :].astype(F32))
        o_ref[rs, :] = _layernorm(v, g, b)
        return carry

    lax.fori_loop(0, nchunks, chunk, 0)

    @pl.when(i == nsteps - 1)
    def _():
        wait_slabs(nslot)


def _combine(dest3, ys, topw, h1, shared, g, b):
    n_tok = h1.shape[0]
    nsteps, _, tm = dest3.shape
    row = lambda i: (i, 0)
    fixed = lambda i: (0, 0)
    return pl.pallas_call(
        functools.partial(_combine_kernel, tm=tm),
        out_shape=jax.ShapeDtypeStruct((n_tok, D_MODEL), F32),
        grid=(nsteps,),
        in_specs=[pl.BlockSpec((1, TOP_K, tm), lambda i: (i, 0, 0), memory_space=pltpu.SMEM),
                  pl.BlockSpec((1, TOP_K, tm), lambda i: (jnp.minimum(i + 1, nsteps - 1), 0, 0),
                               memory_space=pltpu.SMEM),
                  pl.BlockSpec(memory_space=pl.ANY),
                  pl.BlockSpec((tm, TOP_K), row),
                  pl.BlockSpec((tm, D_MODEL), row), pl.BlockSpec((tm, D_MODEL), row),
                  pl.BlockSpec((1, D_MODEL), fixed), pl.BlockSpec((1, D_MODEL), fixed)],
        out_specs=pl.BlockSpec((tm, D_MODEL), row),
        scratch_shapes=[pltpu.VMEM((2, TOP_K, tm, D_MODEL), F32), pltpu.SemaphoreType.DMA((2,))],
        compiler_params=_cparams(("arbitrary",)),
        name="combine_ln2",
    )(dest3, dest3, ys, topw, h1, shared, g, b)


PLAN_TM = 128
LANES = 128


def _small_int_halves(v):
    hi = jnp.floor(v * (1.0 / 256.0))
    return hi.astype(BF16), (v - hi * 256.0).astype(BF16)


def _plan_kernel(ti_ref, dest_ref, info_ref, rank_s, cntc_s, cntr_s, pst_s, *, tm):
    p = pl.program_id(0)
    i = pl.program_id(1)
    ti = ti_ref[...]
    e_iota = lax.broadcasted_iota(I32, (N_EXPERTS, tm), 0)
    reps = tm // LANES

    @pl.when(jnp.logical_and(p == 0, i == 0))
    def _():
        cntc_s[...] = jnp.zeros(cntc_s.shape, F32)
        cntr_s[...] = jnp.zeros(cntr_s.shape, F32)

    @pl.when(p == 0)
    def _():
        hits = [e_iota == ti[k:k + 1] for k in range(TOP_K)]
        m = jnp.where(hits[0], 1.0, 0.0)
        for k in range(1, TOP_K):
            m = m + jnp.where(hits[k], 1.0, 0.0)
        mb = m.astype(BF16)
        earlier = lax.broadcasted_iota(I32, (tm, tm), 0) < lax.broadcasted_iota(I32, (tm, tm), 1)
        pfx = jnp.dot(mb, jnp.where(earlier, 1.0, 0.0).astype(BF16), preferred_element_type=F32)
        val = pfx + jnp.concatenate([cntc_s[...]] * reps, axis=1)
        rank_s[i] = jnp.concatenate(
            [jnp.sum(jnp.where(hits[k], val, 0.0), axis=0, keepdims=True) for k in range(TOP_K)], axis=0)
        cntc_s[...] += jnp.dot(mb, jnp.ones((tm, LANES), BF16), preferred_element_type=F32)
        cntr_s[...] += lax.dot_general(jnp.ones((8, tm), BF16), mb, (((1,), (1,)), ((), ())),
                                       preferred_element_type=F32)

    @pl.when(jnp.logical_and(p == 1, i == 0))
    def _():
        def n_blocks_of(cnt):
            return jnp.right_shift(cnt.astype(I32) + (EXPERT_BLOCK - 1), EXPERT_BLOCK.bit_length() - 1).astype(F32)

        ee0 = lax.broadcasted_iota(I32, (N_EXPERTS, N_EXPERTS), 0)
        ee1 = lax.broadcasted_iota(I32, (N_EXPERTS, N_EXPERTS), 1)
        hi, lo = _small_int_halves(n_blocks_of(cntc_s[...]))
        lower = jnp.where(ee1 < ee0, 1.0, 0.0).astype(BF16)
        bst_c = 256.0 * jnp.dot(lower, hi, preferred_element_type=F32) + jnp.dot(lower, lo, preferred_element_type=F32)
        pst_s[...] = bst_c * float(EXPERT_BLOCK)
        cnt_r = cntr_s[...]
        nb_r = n_blocks_of(cnt_r)
        hi, lo = _small_int_halves(nb_r)
        upper = jnp.where(ee0 < ee1, 1.0, 0.0).astype(BF16)
        bst_r = 256.0 * jnp.dot(hi, upper, preferred_element_type=F32) + jnp.dot(lo, upper, preferred_element_type=F32)
        sel = lax.broadcasted_iota(I32, (8, N_EXPERTS), 0)
        info = jnp.where(sel == 0, cnt_r, jnp.where(sel == 1, nb_r, jnp.where(sel == 2, bst_r, 0.0)))
        info_ref[...] = info.astype(I32)

    @pl.when(p == 1)
    def _():
        pst = jnp.concatenate([pst_s[...]] * reps, axis=1)
        off = jnp.concatenate(
            [jnp.sum(jnp.where(e_iota == ti[k:k + 1], pst, 0.0), axis=0, keepdims=True) for k in range(TOP_K)], axis=0)
        dest_ref[0] = (rank_s[i] + off).astype(I32)


def _plan(topi_t, tm=PLAN_TM):
    n_tok = topi_t.shape[1]
    nt = n_tok // tm
    return pl.pallas_call(
        functools.partial(_plan_kernel, tm=tm),
        out_shape=(jax.ShapeDtypeStruct((nt, TOP_K, tm), I32), jax.ShapeDtypeStruct((8, N_EXPERTS), I32)),
        grid=(2, nt),
        in_specs=[pl.BlockSpec((TOP_K, tm), lambda p, i: (0, i))],
        out_specs=(pl.BlockSpec((1, TOP_K, tm), lambda p, i: (i * p, 0, 0)),
                   pl.BlockSpec((8, N_EXPERTS), lambda p, i: (0, 0))),
        scratch_shapes=[pltpu.VMEM((nt, TOP_K, tm), F32), pltpu.VMEM((N_EXPERTS, LANES), F32),
                        pltpu.VMEM((8, N_EXPERTS), F32), pltpu.VMEM((N_EXPERTS, LANES), F32)],
        compiler_params=_cparams(("arbitrary", "arbitrary")),
        name="expert_row_plan",
    )(topi_t)


def _dispatch_kernel(cnt_ref, nblk_ref, bst_ref, dest_ref, hp_ref, hb_ref, wg_ref, wu_ref, wd_ref, xs_ref, sh_ref,
                     zrow, sem, zsem, tsem, *, tm, epg, tpg, n_blocks):
    i = pl.program_id(0)

    @pl.when(i == 0)
    def _():
        zrow[...] = jnp.zeros(zrow.shape, F32)

    nused = bst_ref[N_EXPERTS - 1] + nblk_ref[N_EXPERTS - 1]
    t_lo = jnp.minimum(nused + i * tpg, n_blocks)
    t_hi = jnp.minimum(nused + (i + 1) * tpg, n_blocks)

    def tail_copy(blk):
        return pltpu.make_async_copy(
            zrow, xs_ref.at[pl.ds(pl.multiple_of(blk * EXPERT_BLOCK, EXPERT_BLOCK), EXPERT_BLOCK), :], tsem)

    def tfill(blk, carry):
        tail_copy(blk).start()
        return carry

    def twait(blk, carry):
        tail_copy(blk).wait()
        return carry

    lax.fori_loop(t_lo, t_hi, tfill, 0)

    def scatter(t, carry):
        for k in range(TOP_K):
            pltpu.make_async_copy(hp_ref.at[pl.ds(t, 1), :], xs_ref.at[pl.ds(dest_ref[0, k, t], 1), :], sem).start()
        return carry

    lax.fori_loop(0, tm, scatter, 0, unroll=2)

    def zfill(r, carry):
        pltpu.make_async_copy(zrow.at[pl.ds(0, 1), :], xs_ref.at[pl.ds(r, 1), :], zsem).start()
        return carry

    def zwait(r, carry):
        pltpu.make_async_copy(zrow.at[pl.ds(0, 1), :], xs_ref.at[pl.ds(0, 1), :], zsem).wait()
        return carry

    bounds = []
    for q in range(epg):
        e = i * epg + q
        lo = bst_ref[e] * EXPERT_BLOCK + cnt_ref[e]
        hi = (bst_ref[e] + nblk_ref[e]) * EXPERT_BLOCK
        bounds.append((lo, hi))
        lax.fori_loop(lo, hi, zfill, 0)

    hb = hb_ref[...]
    hid = jax.nn.silu(jnp.dot(hb, wg_ref[...], preferred_element_type=F32)) * jnp.dot(
        hb, wu_ref[...], preferred_element_type=F32)
    sh_ref[...] = jnp.dot(hid.astype(BF16), wd_ref[...], preferred_element_type=F32).astype(BF16)

    for k in range(TOP_K):
        pltpu.make_async_copy(hp_ref, xs_ref.at[pl.ds(0, tm), :], sem).wait()
    for lo, hi in bounds:
        lax.fori_loop(lo, hi, zwait, 0)
    lax.fori_loop(t_lo, t_hi, twait, 0)


def _dispatch(cnt, nblk, bst, dest3, hp, hb, wgs, wus, wds, n_rows):
    nsteps, _, tm = dest3.shape
    n_tok = hp.shape[0]
    n_blocks = n_rows // EXPERT_BLOCK
    assert N_EXPERTS % nsteps == 0
    row = lambda i, *_: (i, 0)
    fixed = lambda i, *_: (0, 0)
    grid_spec = pltpu.PrefetchScalarGridSpec(
        num_scalar_prefetch=3, grid=(nsteps,),
        in_specs=[pl.BlockSpec((1, TOP_K, tm), lambda i, *_: (i, 0, 0), memory_space=pltpu.SMEM),
                  pl.BlockSpec((tm, D_MODEL), row), pl.BlockSpec((tm, D_MODEL), row),
                  pl.BlockSpec((D_MODEL, D_EXPERT), fixed), pl.BlockSpec((D_MODEL, D_EXPERT), fixed),
                  pl.BlockSpec((D_EXPERT, D_MODEL), fixed)],
        out_specs=(pl.BlockSpec(memory_space=pl.ANY), pl.BlockSpec((tm, D_MODEL), row)),
        scratch_shapes=[pltpu.VMEM((EXPERT_BLOCK, D_MODEL), F32), pltpu.SemaphoreType.DMA(()),
                        pltpu.SemaphoreType.DMA(()), pltpu.SemaphoreType.DMA(())])
    return pl.pallas_call(
        functools.partial(_dispatch_kernel, tm=tm, epg=N_EXPERTS // nsteps, tpg=-(-n_blocks // nsteps),
                          n_blocks=n_blocks),
        out_shape=(jax.ShapeDtypeStruct((n_rows, D_MODEL), F32), jax.ShapeDtypeStruct((n_tok, D_MODEL), BF16)),
        grid_spec=grid_spec,
        compiler_params=_cparams(("arbitrary",)),
        name="dispatch_shared",
    )(cnt, nblk, bst, dest3, hp, hb, wgs, wus, wds)


def _rope_tables(n_pos):
    half = ROT_DIM // 2
    inv_freq = ROPE_THETA ** (-jnp.arange(half, dtype=F32) / half)
    ang = jnp.arange(n_pos).astype(F32)[:, None] * inv_freq[None, :]
    cos, sin = jnp.cos(ang), jnp.sin(ang)
    c = jnp.concatenate([cos, cos, jnp.ones((n_pos, HEAD_DIM - ROT_DIM), F32)], axis=1)
    s1 = jnp.concatenate([-sin, jnp.zeros((n_pos, HEAD_DIM - half), F32)], axis=1)
    s2 = jnp.concatenate([jnp.zeros((n_pos, half), F32), sin, jnp.zeros((n_pos, HEAD_DIM - ROT_DIM), F32)], axis=1)
    return c, s1, s2


def kernel(x, meta_tokens, w_in, conv_w, conv_b, w_rg_a, b_rg_a, w_rg_x, b_rg_x, rg_lambda, attn_sinks, w_o_attn, w_o_lru, w_out, ln1_g, ln1_b, w_router, router_bias, w_gate_e, w_up_e, w_down_e, w_gate_s, w_up_s, w_down_s, ln2_g, ln2_b):
    batch, seq, _ = x.shape
    n_tok = batch * seq
    assert w_in.shape[0] == 1 and seq % 512 == 0 and meta_tokens.shape[0] == N_META
    x2 = x.reshape(n_tok, D_MODEL)
    row = lambda v: v.reshape(1, -1).astype(F32)

    q_end, k_end, v_end = N_Q_HEADS * HEAD_DIM, N_Q_HEADS * HEAD_DIM + KV_DIM, N_Q_HEADS * HEAD_DIM + 2 * KV_DIM
    wi = w_in[0]
    w_in_p = jnp.concatenate([wi[:, :q_end], wi[:, v_end:], wi[:, q_end:k_end], wi[:, k_end:v_end]], axis=1).astype(BF16)
    wax = jnp.concatenate([w_rg_a[0], w_rg_x[0]], axis=-1).astype(BF16)

    c_t, s1_t, s2_t = _rope_tables(N_META + seq)
    tabs_meta = (c_t[:N_META], s1_t[:N_META], s2_t[:N_META])
    tabs_real = (c_t[N_META:], s1_t[N_META:], s2_t[N_META:])

    projm = _matmul(meta_tokens.astype(BF16), w_in_p, BF16, N_META, 1024)
    lru_args = (conv_w[0], row(conv_b[0]), wax, row(b_rg_a[0]), row(b_rg_x[0]), row(rg_lambda[0]))
    _, h_meta = _lru(projm, jnp.zeros((N_META, D_LRU), BF16), jnp.zeros((1, D_LRU), F32), *lru_args,
                     batch=1, seq=N_META, tc=N_META)

    proj = _matmul(x2.astype(BF16), w_in_p, BF16, 1024, 1024)
    o_attn = _attention(proj, projm, attn_sinks[0].astype(F32), tabs_real, tabs_meta, batch, seq)
    y_lru, _ = _lru(proj, projm[:, D_MODEL:2 * D_MODEL], h_meta[0, 7:8], *lru_args, batch=batch, seq=seq, tc=256)
    z = _merge(o_attn, y_lru, w_o_attn[0].astype(BF16), w_o_lru[0].astype(BF16), proj)
    h1, h1b = _ln1(x2, z, w_out[0].astype(BF16), row(ln1_g[0]), row(ln1_b[0]))

    topi_t, topw_t = _router(h1b, w_router[0].T.astype(BF16), router_bias[0].reshape(N_EXPERTS, 1).astype(F32))
    dest3, info = _plan(topi_t)
    cnt, nblk, bst = info[0], info[1], info[2]
    n_rows = -(-(n_tok * TOP_K + N_EXPERTS * (EXPERT_BLOCK - 1)) // EXPERT_BLOCK) * EXPERT_BLOCK
    xs, shared = _dispatch(cnt, nblk, bst, dest3, h1, h1b,w_gate_s[0].astype(BF16), w_up_s[0].astype(BF16),
                           w_down_s[0].astype(BF16), n_rows)
    ys = _experts(nblk, bst, xs, w_gate_e[0], w_up_e[0], w_down_e[0])
    out = _combine(dest3, ys, topw_t.T, h1, shared, row(ln2_g[0]), row(ln2_b[0]))
    return out.reshape(batch, seq, D_MODEL)
```

```python
import functools

import jax
import jax.numpy as jnp
from jax import lax
from jax.experimental import pallas as pl
from jax.experimental.pallas import tpu as pltpu

F32 = jnp.float32
BF16 = jnp.bfloat16
U32 = jnp.uint32
I32 = jnp.int32

D_MODEL = 2048
N_META = 16
N_Q_HEADS = 16
N_KV_HEADS = 4
HEAD_DIM = 128
Q_GROUP = N_Q_HEADS // N_KV_HEADS
ROT_DIM = HEAD_DIM // 4
ROPE_THETA = 500000.0
ATTN_BLOCK = 128
D_LRU = D_MODEL
LRU_BLOCKS = 16
LRU_BW = D_LRU // LRU_BLOCKS
CONV_W = 4
RGLRU_C = 8.0
N_EXPERTS = 256
TOP_K = 8
N_GROUPS = 8
GROUP_SIZE = N_EXPERTS // N_GROUPS
TOPK_GROUPS = 4
D_EXPERT = 512
ROUTED_SCALE = 2.5
EXPERT_BLOCK = 128
DN_ALPHA = 2.0 ** 0.25
LN_EPS = 1e-5
NEG_INF = -1e30
KV_DIM = N_KV_HEADS * HEAD_DIM
Q_DIM = N_Q_HEADS * HEAD_DIM
K_COLBLK = Q_DIM // KV_DIM
V_COLBLK = K_COLBLK + 1
XR_COL = Q_DIM + 2 * KV_DIM
XG_COL = XR_COL + D_LRU
GA_COL = XG_COL + D_LRU
GL_COL = GA_COL + D_MODEL
LRU_HALF = D_LRU // 2
VMEM_LIMIT = 50 * 1024 * 1024


def _cparams(sem):
    return pltpu.CompilerParams(dimension_semantics=sem, vmem_limit_bytes=VMEM_LIMIT)


def _mm_kernel(a_ref, b_ref, o_ref, b_bf):
    @pl.when(pl.program_id(1) == 0)
    def _():
        b_bf[...] = b_ref[...].astype(BF16)

    o_ref[...] = jnp.dot(a_ref[...], b_bf[...], preferred_element_type=F32).astype(o_ref.dtype)


def _matmul(a, b, out_dtype, tm, tn):
    m, k = a.shape
    n = b.shape[1]
    return pl.pallas_call(
        _mm_kernel,
        out_shape=jax.ShapeDtypeStruct((m, n), out_dtype),
        grid=(n // tn, m // tm),
        in_specs=[pl.BlockSpec((tm, k), lambda j, i: (i, 0)),
                  pl.BlockSpec((k, tn), lambda j, i: (0, j))],
        out_specs=pl.BlockSpec((tm, tn), lambda j, i: (i, j)),
        scratch_shapes=[pltpu.VMEM((k, tn), BF16)],
        compiler_params=_cparams(("arbitrary", "arbitrary")),
        name="dense_matmul",
    )(a, b)


def _rope(x, rot, c, s):
    return x.astype(F32) * c + jnp.dot(x, rot, preferred_element_type=F32) * s


def _attn_kernel(sink_ref, q_ref, kc_ref, kp_ref, vc_ref, vp_ref, km_ref, vm_ref,
                 cq_ref, sq_ref, cp_ref, sp_ref, cm_ref, sm_ref, o_ref, *, tq):
    n = pl.program_id(1)
    nsub = tq // ATTN_BLOCK
    scale = HEAD_DIM ** -0.5
    cq, sq = cq_ref[...], sq_ref[...]
    half = ROT_DIM // 2
    rj = lax.broadcasted_iota(I32, (HEAD_DIM, HEAD_DIM), 0)
    ri = lax.broadcasted_iota(I32, (HEAD_DIM, HEAD_DIM), 1)
    rot = (jnp.where(jnp.logical_and(ri < half, rj == ri + half), -1.0, 0.0)
           + jnp.where(jnp.logical_and(jnp.logical_and(ri >= half, ri < ROT_DIM), rj == ri - half), 1.0, 0.0)
           ).astype(BF16)
    rows = Q_GROUP * ATTN_BLOCK
    r_idx = lax.broadcasted_iota(I32, (rows, ATTN_BLOCK), 0) % ATTN_BLOCK
    c_idx = lax.broadcasted_iota(I32, (rows, ATTN_BLOCK), 1)
    cur_ok = c_idx <= r_idx
    prev_ok = c_idx > r_idx
    first_prev_ok = (c_idx - r_idx) > jnp.where(n > 0, 0, ATTN_BLOCK)
    meta_ok = c_idx < N_META
    pad_rows = jnp.zeros((ATTN_BLOCK - N_META, HEAD_DIM), BF16)
    ones_blk = jnp.ones((3 * ATTN_BLOCK, HEAD_DIM), BF16)
    dn = (((1,), (1,)), ((), ()))
    for g in range(N_KV_HEADS):
        hs = slice(g * HEAD_DIM, (g + 1) * HEAD_DIM)
        k_g = _rope(kc_ref[:, hs], rot, cq, sq).astype(BF16)
        kp_g = _rope(kp_ref[:, hs], rot, cp_ref[...], sp_ref[...]).astype(BF16)
        km_g = jnp.concatenate([_rope(km_ref[:, hs], rot, cm_ref[...], sm_ref[...]).astype(BF16), pad_rows], axis=0)
        v_g = vc_ref[:, hs]
        vp_g = vp_ref[:, hs]
        vm_g = jnp.concatenate([vm_ref[:, hs], pad_rows], axis=0)
        sink = jnp.concatenate(
            [jnp.full((ATTN_BLOCK, 1), sink_ref[g * Q_GROUP + j], F32) for j in range(Q_GROUP)], axis=0)
        for sub in range(nsub):
            rs = slice(sub * ATTN_BLOCK, (sub + 1) * ATTN_BLOCK)
            q_st = jnp.concatenate(
                [_rope(q_ref[rs, (g * Q_GROUP + j) * HEAD_DIM:(g * Q_GROUP + j + 1) * HEAD_DIM], rot,
                       cq[rs], sq[rs]) for j in range(Q_GROUP)], axis=0).astype(BF16)
            if sub == 0:
                k_all = jnp.concatenate([kp_g, k_g[rs], km_g], axis=0)
                v_all = jnp.concatenate([vp_g, v_g[rs], vm_g], axis=0)
                p_ok = first_prev_ok
            else:
                both = slice((sub - 1) * ATTN_BLOCK, (sub + 1) * ATTN_BLOCK)
                k_all = jnp.concatenate([k_g[both], km_g], axis=0)
                v_all = jnp.concatenate([v_g[both], vm_g], axis=0)
                p_ok = prev_ok
            sc = lax.dot_general(q_st, k_all, dn, preferred_element_type=F32) * scale
            s_p = jnp.where(p_ok, sc[:, :ATTN_BLOCK], NEG_INF)
            s_c = jnp.where(cur_ok, sc[:, ATTN_BLOCK:2 * ATTN_BLOCK], NEG_INF)
            s_m = jnp.where(meta_ok, sc[:, 2 * ATTN_BLOCK:], NEG_INF)
            m = jnp.maximum(jnp.max(jnp.maximum(jnp.maximum(s_p, s_c), s_m), axis=1, keepdims=True), sink)
            p = jnp.concatenate([jnp.exp(s_p - m), jnp.exp(s_c - m), jnp.exp(s_m - m)], axis=1).astype(BF16)
            od = jnp.dot(p, jnp.concatenate([v_all, ones_blk], axis=1), preferred_element_type=F32)
            o = od[:, :HEAD_DIM] / (od[:, HEAD_DIM:] + jnp.exp(sink - m))
            for j in range(Q_GROUP):
                h = g * Q_GROUP + j
                o_ref[rs, h * HEAD_DIM:(h + 1) * HEAD_DIM] = o[j * ATTN_BLOCK:(j + 1) * ATTN_BLOCK].astype(BF16)


def _attention(proj, projm, sinks, tabs_real, tabs_meta, batch, seq, tq=512):
    n_tok = batch * seq
    nq = seq // tq
    sub_per = tq // ATTN_BLOCK

    def prev_blk(b, n):
        return jnp.maximum(b * (seq // ATTN_BLOCK) + n * sub_per - 1, 0)

    tab_spec = pl.BlockSpec((tq, HEAD_DIM), lambda b, n, s: (n, 0))
    tabp_spec = pl.BlockSpec((ATTN_BLOCK, HEAD_DIM), lambda b, n, s: (jnp.maximum(n * sub_per - 1, 0), 0))
    tabm_spec = pl.BlockSpec((N_META, HEAD_DIM), lambda b, n, s: (0, 0))
    in_specs = [
        pl.BlockSpec((tq, D_MODEL), lambda b, n, s: (b * nq + n, 0)),
        pl.BlockSpec((tq, KV_DIM), lambda b, n, s: (b * nq + n, K_COLBLK)),
        pl.BlockSpec((ATTN_BLOCK, KV_DIM), lambda b, n, s: (prev_blk(b, n), K_COLBLK)),
        pl.BlockSpec((tq, KV_DIM), lambda b, n, s: (b * nq + n, V_COLBLK)),
        pl.BlockSpec((ATTN_BLOCK, KV_DIM), lambda b, n, s: (prev_blk(b, n), V_COLBLK)),
        pl.BlockSpec((N_META, KV_DIM), lambda b, n, s: (0, K_COLBLK)),
        pl.BlockSpec((N_META, KV_DIM), lambda b, n, s: (0, V_COLBLK)),
        tab_spec, tab_spec, tabp_spec, tabp_spec, tabm_spec, tabm_spec,
    ]
    grid_spec = pltpu.PrefetchScalarGridSpec(
        num_scalar_prefetch=1, grid=(batch, nq), in_specs=in_specs,
        out_specs=pl.BlockSpec((tq, D_MODEL), lambda b, n, s: (b * nq + n, 0)))
    return pl.pallas_call(
        functools.partial(_attn_kernel, tq=tq),
        out_shape=jax.ShapeDtypeStruct((n_tok, D_MODEL), BF16),
        grid_spec=grid_spec,
        compiler_params=_cparams(("parallel", "parallel")),
        name="swa_attention",
    )(sinks, proj, proj, proj, proj, proj, projm, projm,
      tabs_real[0], tabs_real[1], tabs_real[0], tabs_real[1], tabs_meta[0], tabs_meta[1])


def _softplus(z):
    return jnp.maximum(z, 0.0) + jnp.log1p(jnp.exp(-jnp.abs(z)))


def _lru_kernel(xr_ref, prev_ref, hist0_ref, xg_ref, cw_ref, cb_ref, wax_ref, ba_ref, bx_ref, lam_ref, h0_ref,
                y_ref, hl_ref, a_s, u_s, hcar, *, tc):
    c = pl.program_id(2)
    width = a_s.shape[1]
    blocks = width // LRU_BW

    @pl.when(c == 0)
    def _():
        hcar[...] = jnp.broadcast_to(h0_ref[...], hcar.shape)

    first = c == 0
    for nb in range(blocks):
        cs = slice(nb * LRU_BW, (nb + 1) * LRU_BW)
        hist = jnp.where(first, hist0_ref[:, cs], prev_ref[:, cs]).astype(F32)
        x = xr_ref[:, cs].astype(F32)
        ext = jnp.concatenate([hist, x], axis=0)
        cw = cw_ref[:, cs]
        y = cb_ref[:, cs] + x * cw[CONV_W - 1:CONV_W]
        for d in range(1, CONV_W):
            y = y + pltpu.roll(ext, d, 0)[N_META:] * cw[CONV_W - 1 - d:CONV_W - d]
        gates = jnp.dot(y.astype(BF16), wax_ref[nb], preferred_element_type=F32)
        r = jax.nn.sigmoid(gates[:, :LRU_BW] + ba_ref[:, cs])
        ig = jax.nn.sigmoid(gates[:, LRU_BW:] + bx_ref[:, cs])
        log_a = (-RGLRU_C) * r * _softplus(-lam_ref[:, cs])
        a = jnp.exp(log_a)
        a_s[:, cs] = a
        u_s[:, cs] = jnp.sqrt(-jnp.tanh(log_a) * (a * a + 1.0)) * (ig * y)

    row = lax.broadcasted_iota(I32, (8, width), 0)

    def body(i, h):
        sl = pl.ds(pl.multiple_of(i * 8, 8), 8)
        a = a_s[sl, :]
        u = u_s[sl, :]
        for d in (1, 2, 4):
            ok = row >= d
            u = jnp.where(ok, a * pltpu.roll(u, d, 0) + u, u)
            a = jnp.where(ok, a * pltpu.roll(a, d, 0), a)
        hs = a * h + u
        u_s[sl, :] = hs
        return jnp.broadcast_to(hs[7:8, :], (8, width))

    h_fin = lax.fori_loop(0, tc // 8, body, hcar[...])
    hcar[...] = h_fin
    hl_ref[0] = h_fin
    for nb in range(blocks):
        cs = slice(nb * LRU_BW, (nb + 1) * LRU_BW)
        y_ref[:, cs] = (u_s[:, cs] * jax.nn.gelu(xg_ref[:, cs].astype(F32))).astype(BF16)


def _lru(proj, hist0, h0, cw, cb, wax, ba, bx, lam, batch, seq, tc):
    nch = seq // tc
    per16 = tc // N_META

    xr_blk = XR_COL // LRU_HALF
    xg_blk = XG_COL // LRU_HALF

    def prev_idx(b, h, c):
        return (jnp.maximum(b * (seq // N_META) + c * per16 - 1, 0), xr_blk + h)

    vec = pl.BlockSpec((1, LRU_HALF), lambda b, h, c: (0, h))
    in_specs = [
        pl.BlockSpec((tc, LRU_HALF), lambda b, h, c: (b * nch + c, xr_blk + h)),
        pl.BlockSpec((N_META, LRU_HALF), prev_idx),
        pl.BlockSpec((N_META, LRU_HALF), lambda b, h, c: (0, h)),
        pl.BlockSpec((tc, LRU_HALF), lambda b, h, c: (b * nch + c, xg_blk + h)),
        pl.BlockSpec((CONV_W, LRU_HALF), lambda b, h, c: (0, h)),
        vec,
        pl.BlockSpec((LRU_BLOCKS // 2, LRU_BW, 2 * LRU_BW), lambda b, h, c: (h, 0, 0)),
        vec, vec, vec, vec,
    ]
    return pl.pallas_call(
        functools.partial(_lru_kernel, tc=tc),
        out_shape=(jax.ShapeDtypeStruct((batch * seq, D_LRU), BF16),
                   jax.ShapeDtypeStruct((batch, 8, D_LRU), F32)),
        grid=(batch, 2, nch),
        in_specs=in_specs,
        out_specs=(pl.BlockSpec((tc, LRU_HALF), lambda b, h, c: (b * nch + c, h)),
                   pl.BlockSpec((1, 8, LRU_HALF), lambda b, h, c: (b, 0, h))),
        scratch_shapes=[pltpu.VMEM((tc, LRU_HALF), F32), pltpu.VMEM((tc, LRU_HALF), F32),
                        pltpu.VMEM((8, LRU_HALF), F32)],
        compiler_params=_cparams(("arbitrary", "arbitrary", "arbitrary")),
        name="conv_rglru",
    )(proj, proj, hist0, proj, cw, cb, wax, ba, bx, lam, h0)


def _merge_kernel(o_ref, y_ref, wa_ref, wl_ref, ga_ref, gl_ref, z_ref):
    ya = jnp.dot(o_ref[...], wa_ref[...], preferred_element_type=F32)
    yl = jnp.dot(y_ref[...], wl_ref[...], preferred_element_type=F32)
    z = jax.nn.sigmoid(ga_ref[...].astype(F32)) * ya + jax.nn.sigmoid(gl_ref[...].astype(F32)) * yl
    z_ref[...] = z.astype(BF16)


def _merge(o_attn, y_lru, wa, wl, proj, tm=512, tn=1024):
    n_tok = o_attn.shape[0]
    ga_blk = GA_COL // tn
    gl_blk = GL_COL // tn
    return pl.pallas_call(
        _merge_kernel,
        out_shape=jax.ShapeDtypeStruct((n_tok, D_MODEL), BF16),
        grid=(D_MODEL // tn, n_tok // tm),
        in_specs=[pl.BlockSpec((tm, D_MODEL), lambda j, i: (i, 0)),
                  pl.BlockSpec((tm, D_MODEL), lambda j, i: (i, 0)),
                  pl.BlockSpec((D_MODEL, tn), lambda j, i: (0, j)),
                  pl.BlockSpec((D_MODEL, tn), lambda j, i: (0, j)),
                  pl.BlockSpec((tm, tn), lambda j, i: (i, ga_blk + j)),
                  pl.BlockSpec((tm, tn), lambda j, i: (i, gl_blk + j))],
        out_specs=pl.BlockSpec((tm, tn), lambda j, i: (i, j)),
        compiler_params=_cparams(("parallel", "parallel")),
        name="gated_merge",
    )(o_attn, y_lru, wa, wl, proj, proj)


def _layernorm(v, g, b):
    mu = jnp.mean(v, axis=-1, keepdims=True)
    dv = v - mu
    var = jnp.mean(dv * dv, axis=-1, keepdims=True)
    return dv * lax.rsqrt(var + LN_EPS) * g + b


def _ln1_kernel(x_ref, z_ref, w_ref, g_ref, b_ref, h_ref, hb_ref):
    mix = jnp.dot(z_ref[...], w_ref[...], preferred_element_type=F32)
    h = _layernorm(DN_ALPHA * x_ref[...] + mix, g_ref[...], b_ref[...])
    h_ref[...] = h
    hb_ref[...] = h.astype(BF16)


def _ln1(x2, z, w_out, g, b, tm=256):
    n_tok = x2.shape[0]
    row = lambda i: (i, 0)
    fixed = lambda i: (0, 0)
    return pl.pallas_call(
        _ln1_kernel,
        out_shape=(jax.ShapeDtypeStruct((n_tok, D_MODEL), F32),
                   jax.ShapeDtypeStruct((n_tok, D_MODEL), BF16)),
        grid=(n_tok // tm,),
        in_specs=[pl.BlockSpec((tm, D_MODEL), row), pl.BlockSpec((tm, D_MODEL), row),
                  pl.BlockSpec((D_MODEL, D_MODEL), fixed),
                  pl.BlockSpec((1, D_MODEL), fixed), pl.BlockSpec((1, D_MODEL), fixed)],
        out_specs=(pl.BlockSpec((tm, D_MODEL), row), pl.BlockSpec((tm, D_MODEL), row)),
        compiler_params=_cparams(("parallel",)),
        name="outproj_ln1",
    )(x2, z, w_out, g, b)


def _router_kernel(h_ref, w_ref, bias_ref, ti_ref, tw_ref, *, tm):
    logits = lax.dot_general(w_ref[...], h_ref[...], (((1,), (1,)), ((), ())), preferred_element_type=F32)
    scores = jax.nn.sigmoid(logits)
    sel = scores + bias_ref[...]
    neg = -jnp.inf
    e_iota = lax.broadcasted_iota(I32, (N_EXPERTS, tm), 0).astype(F32)
    g_iota32 = lax.broadcasted_iota(I32, (GROUP_SIZE, tm), 0).astype(F32)
    gs_rows = []
    for g in range(N_GROUPS):
        blk = sel[g * GROUP_SIZE:(g + 1) * GROUP_SIZE]
        m1 = jnp.max(blk, axis=0, keepdims=True)
        i1 = jnp.min(jnp.where(blk == m1, g_iota32, float(GROUP_SIZE)), axis=0, keepdims=True)
        m2 = jnp.max(jnp.where(g_iota32 == i1, neg, blk), axis=0, keepdims=True)
        gs_rows.append(m1 + m2)
    gs = jnp.concatenate(gs_rows, axis=0)
    g_iota = lax.broadcasted_iota(I32, (N_GROUPS, tm), 0).astype(F32)
    gsel = jnp.zeros((N_GROUPS, tm), F32)
    for _ in range(TOPK_GROUPS):
        m = jnp.max(gs, axis=0, keepdims=True)
        idx = jnp.min(jnp.where(gs == m, g_iota, float(N_GROUPS)), axis=0, keepdims=True)
        hit = g_iota == idx
        gsel = jnp.where(hit, 1.0, gsel)
        gs = jnp.where(hit, neg, gs)
    masked = jnp.concatenate(
        [jnp.where(gsel[g:g + 1] > 0.0, sel[g * GROUP_SIZE:(g + 1) * GROUP_SIZE], neg) for g in range(N_GROUPS)],
        axis=0)
    tis, tws = [], []
    for _ in range(TOP_K):
        m = jnp.max(masked, axis=0, keepdims=True)
        idx = jnp.min(jnp.where(masked == m, e_iota, float(N_EXPERTS)), axis=0, keepdims=True)
        hit = e_iota == idx
        tis.append(idx)
        tws.append(jnp.sum(jnp.where(hit, scores, 0.0), axis=0, keepdims=True))
        masked = jnp.where(hit, neg, masked)
    tw = jnp.concatenate(tws, axis=0)
    ti_ref[...] = jnp.concatenate(tis, axis=0).astype(I32)
    tw_ref[...] = tw / jnp.sum(tw, axis=0, keepdims=True) * ROUTED_SCALE


def _router(hb, wr_t, bias, tm=512):
    n_tok = hb.shape[0]
    return pl.pallas_call(
        functools.partial(_router_kernel, tm=tm),
        out_shape=(jax.ShapeDtypeStruct((TOP_K, n_tok), I32), jax.ShapeDtypeStruct((TOP_K, n_tok), F32)),
        grid=(n_tok // tm,),
        in_specs=[pl.BlockSpec((tm, D_MODEL), lambda i: (i, 0)),
                  pl.BlockSpec((N_EXPERTS, D_MODEL), lambda i: (0, 0)),
                  pl.BlockSpec((N_EXPERTS, 1), lambda i: (0, 0))],
        out_specs=(pl.BlockSpec((TOP_K, tm), lambda i: (0, i)), pl.BlockSpec((TOP_K, tm), lambda i: (0, i))),
        compiler_params=_cparams(("parallel",)),
        name="router_topk",
    )(hb, wr_t, bias)


def _expert_kernel(nblk_ref, bst_ref, xs_ref, wg_ref, wu_ref, wd_ref, ys_ref,
                   xbuf, obuf, isem, osem, wgb, wub, wdb):
    e = pl.program_id(0)
    nb = nblk_ref[e]
    b0 = bst_ref[e]
    nused = bst_ref[N_EXPERTS - 1] + nblk_ref[N_EXPERTS - 1]

    def rows(g):
        return pl.ds(pl.multiple_of(g * EXPERT_BLOCK, EXPERT_BLOCK), EXPERT_BLOCK)

    def in_copy(g, slot):
        return pltpu.make_async_copy(xs_ref.at[rows(g), :], xbuf.at[slot], isem.at[slot])

    def out_copy(g, slot):
        return pltpu.make_async_copy(obuf.at[slot], ys_ref.at[rows(g), :], osem.at[slot])

    @pl.when(nb > 0)
    def _():
        @pl.when(b0 == 0)
        def _():
            in_copy(0, 0).start(priority=1)

        wgb[...] = wg_ref[0].astype(BF16)
        wub[...] = wu_ref[0].astype(BF16)
        wdb[...] = wd_ref[0].astype(BF16)

        def body(j, carry):
            g = b0 + j
            slot = g & 1

            @pl.when(g + 1 < nused)
            def _():
                in_copy(g + 1, 1 - slot).start(priority=1)

            in_copy(g, slot).wait()
            x = xbuf[slot].astype(BF16)
            hid =jax.nn.silu(jnp.dot(x, wgb[...], preferred_element_type=F32)) * jnp.dot(
                x, wub[...], preferred_element_type=F32)
            out = jnp.dot(hid.astype(BF16), wdb[...], preferred_element_type=F32)

            @pl.when(g >= 2)
            def _():
                out_copy(g - 2, slot).wait()

            obuf[slot] = out
            out_copy(g, slot).start(priority=1)
            return carry

        lax.fori_loop(0, nb, body, 0)

    @pl.when(e == N_EXPERTS - 1)
    def _():
        @pl.when(nused >= 2)
        def _():
            out_copy(nused - 2, nused & 1).wait()

        out_copy(nused - 1, (nused - 1) & 1).wait()


def _experts(nblk, bst, xs, wg, wu, wd):
    grid_spec = pltpu.PrefetchScalarGridSpec(
        num_scalar_prefetch=2, grid=(N_EXPERTS,),
        in_specs=[pl.BlockSpec(memory_space=pl.ANY),
                  pl.BlockSpec((1, D_MODEL, D_EXPERT), lambda e, nb, bs: (e, 0, 0)),
                  pl.BlockSpec((1, D_MODEL, D_EXPERT), lambda e, nb, bs: (e, 0, 0)),
                  pl.BlockSpec((1, D_EXPERT, D_MODEL), lambda e, nb, bs: (e, 0, 0))],
        out_specs=pl.BlockSpec(memory_space=pl.ANY),
        scratch_shapes=[pltpu.VMEM((2, EXPERT_BLOCK, D_MODEL), F32), pltpu.VMEM((2, EXPERT_BLOCK, D_MODEL), F32),
                        pltpu.SemaphoreType.DMA((2,)), pltpu.SemaphoreType.DMA((2,)),
                        pltpu.VMEM((D_MODEL, D_EXPERT), BF16), pltpu.VMEM((D_MODEL, D_EXPERT), BF16),
                        pltpu.VMEM((D_EXPERT, D_MODEL), BF16)])
    return pl.pallas_call(
        _expert_kernel,
        out_shape=jax.ShapeDtypeStruct(xs.shape, F32),
        grid_spec=grid_spec,
        input_output_aliases={2: 0},
        compiler_params=_cparams(("arbitrary",)),
        name="routed_experts",
    )(nblk, bst, xs, wg, wu, wd)


COMBINE_CHUNK = 16


def _combine_kernel(dc_ref, dn_ref, ys_ref, tw_ref, h_ref, sh_ref, g_ref, b_ref, o_ref, gbuf, sem, *, tm):
    i = pl.program_id(0)
    nsteps = pl.num_programs(0)
    nchunks = tm // COMBINE_CHUNK

    def issue_tile(d_ref, s):
        def body(t, carry):
            for k in range(TOP_K):
                pltpu.make_async_copy(ys_ref.at[pl.ds(d_ref[0, k, t], 1), :],
                                      gbuf.at[s, k, pl.ds(t, 1), :], sem.at[s]).start(priority=k % 2)
            return carry
        lax.fori_loop(0, tm, body, 0, unroll=2)

    def wait_slabs(s):
        for k in range(TOP_K):
            pltpu.make_async_copy(ys_ref.at[pl.ds(0, tm), :], gbuf.at[s, k], sem.at[s]).wait()

    @pl.when(i == 0)
    def _():
        issue_tile(dc_ref, 0)

    g = g_ref[...]
    b = b_ref[...]

    def reduce_tile(s):
        def chunk(c, carry):
            rs = pl.ds(pl.multiple_of(c * COMBINE_CHUNK, COMBINE_CHUNK), COMBINE_CHUNK)
            tw = tw_ref[rs, :]
            routed = tw[:, 0:1] * gbuf[s, 0, rs, :]
            for k in range(1, TOP_K):
                routed = routed + tw[:, k:k + 1] * gbuf[s, k, rs, :]
            v = DN_ALPHA * h_ref[rs, :] + (routed + sh_ref[rs, :].astype(F32))
            o_ref[rs, :] = _layernorm(v, g, b)
            return carry
        lax.fori_loop(0, nchunks, chunk, 0)

    for s in (0, 1):
        @pl.when((i & 1) == s)
        def _(s=s):
            @pl.when(i + 1 < nsteps)
            def _():
                issue_tile(dn_ref, 1 - s)

            wait_slabs(s)
            reduce_tile(s)


def _combine(dest3, ys, topw, h1, shared, g, b):
    n_tok = h1.shape[0]
    nsteps, _, tm = dest3.shape
    row = lambda i: (i, 0)
    fixed = lambda i: (0, 0)
    return pl.pallas_call(
        functools.partial(_combine_kernel, tm=tm),
        out_shape=jax.ShapeDtypeStruct((n_tok, D_MODEL), F32),
        grid=(nsteps,),
        in_specs=[pl.BlockSpec((1, TOP_K, tm), lambda i: (i, 0, 0), memory_space=pltpu.SMEM),
                  pl.BlockSpec((1, TOP_K, tm), lambda i: (jnp.minimum(i + 1, nsteps - 1), 0, 0),
                               memory_space=pltpu.SMEM),
                  pl.BlockSpec(memory_space=pl.ANY),
                  pl.BlockSpec((tm, TOP_K), row),
                  pl.BlockSpec((tm, D_MODEL), row), pl.BlockSpec((tm, D_MODEL), row),
                  pl.BlockSpec((1, D_MODEL), fixed), pl.BlockSpec((1, D_MODEL), fixed)],
        out_specs=pl.BlockSpec((tm, D_MODEL), row),
        scratch_shapes=[pltpu.VMEM((2, TOP_K, tm, D_MODEL), F32), pltpu.SemaphoreType.DMA((2,))],
        compiler_params=_cparams(("arbitrary",)),
        name="combine_ln2",
    )(dest3, dest3, ys, topw, h1, shared, g, b)


PLAN_TM = 128
LANES = 128


def _small_int_halves(v):
    hi = jnp.floor(v * (1.0 / 256.0))
    return hi.astype(BF16), (v - hi * 256.0).astype(BF16)


def _plan_kernel(ti_ref, dest_ref, info_ref, rank_s, cntc_s, cntr_s, pst_s, *, tm):
    p = pl.program_id(0)
    i = pl.program_id(1)
    ti = ti_ref[...]
    e_iota = lax.broadcasted_iota(I32, (N_EXPERTS, tm), 0)
    reps = tm // LANES

    @pl.when(jnp.logical_and(p == 0, i == 0))
    def _():
        cntc_s[...] = jnp.zeros(cntc_s.shape, F32)
        cntr_s[...] = jnp.zeros(cntr_s.shape, F32)

    @pl.when(p == 0)
    def _():
        hits = [e_iota == ti[k:k + 1] for k in range(TOP_K)]
        m = jnp.where(hits[0], 1.0, 0.0)
        for k in range(1, TOP_K):
            m = m + jnp.where(hits[k], 1.0, 0.0)
        mb = m.astype(BF16)
        earlier = lax.broadcasted_iota(I32, (tm, tm), 0) < lax.broadcasted_iota(I32, (tm, tm), 1)
        pfx = jnp.dot(mb, jnp.where(earlier, 1.0, 0.0).astype(BF16), preferred_element_type=F32)
        val = pfx + jnp.concatenate([cntc_s[...]] * reps, axis=1)
        rank_s[i] = jnp.concatenate(
            [jnp.sum(jnp.where(hits[k], val, 0.0), axis=0, keepdims=True) for k in range(TOP_K)], axis=0)
        cntc_s[...] += jnp.dot(mb, jnp.ones((tm, LANES), BF16), preferred_element_type=F32)
        cntr_s[...] += lax.dot_general(jnp.ones((8, tm), BF16), mb, (((1,), (1,)), ((), ())),
                                       preferred_element_type=F32)

    @pl.when(jnp.logical_and(p == 1, i == 0))
    def _():
        def n_blocks_of(cnt):
            return jnp.right_shift(cnt.astype(I32) + (EXPERT_BLOCK - 1), EXPERT_BLOCK.bit_length() - 1).astype(F32)

        ee0 = lax.broadcasted_iota(I32, (N_EXPERTS, N_EXPERTS), 0)
        ee1 = lax.broadcasted_iota(I32, (N_EXPERTS, N_EXPERTS), 1)
        hi, lo = _small_int_halves(n_blocks_of(cntc_s[...]))
        lower = jnp.where(ee1 < ee0, 1.0, 0.0).astype(BF16)
        bst_c = 256.0 * jnp.dot(lower, hi, preferred_element_type=F32) + jnp.dot(lower, lo, preferred_element_type=F32)
        pst_s[...] = bst_c * float(EXPERT_BLOCK)
        cnt_r = cntr_s[...]
        nb_r = n_blocks_of(cnt_r)
        hi, lo = _small_int_halves(nb_r)
        upper = jnp.where(ee0 < ee1, 1.0, 0.0).astype(BF16)
        bst_r = 256.0 * jnp.dot(hi, upper, preferred_element_type=F32) + jnp.dot(lo, upper, preferred_element_type=F32)
        sel = lax.broadcasted_iota(I32, (8, N_EXPERTS), 0)
        info = jnp.where(sel == 0, cnt_r, jnp.where(sel == 1, nb_r, jnp.where(sel == 2, bst_r, 0.0)))
        info_ref[...] = info.astype(I32)

    @pl.when(p == 1)
    def _():
        pst = jnp.concatenate([pst_s[...]] * reps, axis=1)
        off = jnp.concatenate(
            [jnp.sum(jnp.where(e_iota == ti[k:k + 1], pst, 0.0), axis=0, keepdims=True) for k in range(TOP_K)], axis=0)
        dest_ref[0] = (rank_s[i] + off).astype(I32)


def _plan(topi_t, tm=PLAN_TM):
    n_tok = topi_t.shape[1]
    nt = n_tok // tm
    return pl.pallas_call(
        functools.partial(_plan_kernel, tm=tm),
        out_shape=(jax.ShapeDtypeStruct((nt, TOP_K, tm), I32), jax.ShapeDtypeStruct((8, N_EXPERTS), I32)),
        grid=(2, nt),
        in_specs=[pl.BlockSpec((TOP_K, tm), lambda p, i: (0, i))],
        out_specs=(pl.BlockSpec((1, TOP_K, tm), lambda p, i: (i * p, 0, 0)),
                   pl.BlockSpec((8, N_EXPERTS), lambda p, i: (0, 0))),
        scratch_shapes=[pltpu.VMEM((nt, TOP_K, tm), F32), pltpu.VMEM((N_EXPERTS, LANES), F32),
                        pltpu.VMEM((8, N_EXPERTS), F32), pltpu.VMEM((N_EXPERTS, LANES), F32)],
        compiler_params=_cparams(("arbitrary", "arbitrary")),
        name="expert_row_plan",
    )(topi_t)


def _dispatch_kernel(cnt_ref, nblk_ref, bst_ref, dest_ref, hp_ref, hb_ref, wg_ref, wu_ref, wd_ref, xs_ref, sh_ref,
                     zrow, sem, zsem, tsem, *, tm, epg, tpg, n_blocks):
    i = pl.program_id(0)

    @pl.when(i == 0)
    def _():
        zrow[...] = jnp.zeros(zrow.shape, F32)

    nused = bst_ref[N_EXPERTS - 1] + nblk_ref[N_EXPERTS - 1]
    t_lo = jnp.minimum(nused + i * tpg, n_blocks)
    t_hi = jnp.minimum(nused + (i + 1) * tpg, n_blocks)

    def tail_copy(blk):
        return pltpu.make_async_copy(
            zrow, xs_ref.at[pl.ds(pl.multiple_of(blk * EXPERT_BLOCK, EXPERT_BLOCK), EXPERT_BLOCK), :], tsem)

    def tfill(blk, carry):
        tail_copy(blk).start()
        return carry

    def twait(blk, carry):
        tail_copy(blk).wait()
        return carry

    lax.fori_loop(t_lo, t_hi, tfill, 0)

    def scatter(t, carry):
        for k in range(TOP_K):
            pltpu.make_async_copy(hp_ref.at[pl.ds(t, 1), :], xs_ref.at[pl.ds(dest_ref[0, k, t], 1), :],
                                  sem).start(priority=k % 2)
        return carry

    lax.fori_loop(0, tm, scatter, 0, unroll=2)

    def zfill(r, carry):
        pltpu.make_async_copy(zrow.at[pl.ds(0, 1), :], xs_ref.at[pl.ds(r, 1), :], zsem).start()
        return carry

    def zwait(r, carry):
        pltpu.make_async_copy(zrow.at[pl.ds(0, 1), :], xs_ref.at[pl.ds(0, 1), :], zsem).wait()
        return carry

    bounds = []
    for q in range(epg):
        e = i * epg + q
        lo = bst_ref[e] * EXPERT_BLOCK + cnt_ref[e]
        hi = (bst_ref[e] + nblk_ref[e]) * EXPERT_BLOCK
        bounds.append((lo, hi))
        lax.fori_loop(lo, hi, zfill, 0)

    hb = hb_ref[...]
    hid = jax.nn.silu(jnp.dot(hb, wg_ref[...], preferred_element_type=F32)) * jnp.dot(
        hb, wu_ref[...], preferred_element_type=F32)
    sh_ref[...] = jnp.dot(hid.astype(BF16), wd_ref[...], preferred_element_type=F32).astype(BF16)

    for k in range(TOP_K):
        pltpu.make_async_copy(hp_ref, xs_ref.at[pl.ds(0, tm), :], sem).wait()
    for lo, hi in bounds:
        lax.fori_loop(lo, hi, zwait, 0)
    lax.fori_loop(t_lo, t_hi, twait, 0)


def _dispatch(cnt, nblk, bst, dest3, hp, hb, wgs, wus, wds, n_rows):
    nsteps, _, tm = dest3.shape
    n_tok = hp.shape[0]
    n_blocks = n_rows // EXPERT_BLOCK
    assert N_EXPERTS % nsteps == 0
    row = lambda i, *_: (i, 0)
    fixed = lambda i, *_: (0, 0)
    grid_spec = pltpu.PrefetchScalarGridSpec(
        num_scalar_prefetch=3, grid=(nsteps,),
        in_specs=[pl.BlockSpec((1, TOP_K, tm), lambda i, *_: (i, 0, 0), memory_space=pltpu.SMEM),
                  pl.BlockSpec((tm, D_MODEL), row), pl.BlockSpec((tm, D_MODEL), row),
                  pl.BlockSpec((D_MODEL, D_EXPERT), fixed), pl.BlockSpec((D_MODEL, D_EXPERT), fixed),
                  pl.BlockSpec((D_EXPERT, D_MODEL), fixed)],
        out_specs=(pl.BlockSpec(memory_space=pl.ANY), pl.BlockSpec((tm, D_MODEL), row)),
        scratch_shapes=[pltpu.VMEM((EXPERT_BLOCK, D_MODEL), F32), pltpu.SemaphoreType.DMA(()),
                        pltpu.SemaphoreType.DMA(()), pltpu.SemaphoreType.DMA(())])
    return pl.pallas_call(
        functools.partial(_dispatch_kernel, tm=tm, epg=N_EXPERTS // nsteps, tpg=-(-n_blocks // nsteps),
                          n_blocks=n_blocks),
        out_shape=(jax.ShapeDtypeStruct((n_rows, D_MODEL), F32), jax.ShapeDtypeStruct((n_tok, D_MODEL), BF16)),
        grid_spec=grid_spec,
        compiler_params=_cparams(("arbitrary",)),
        name="dispatch_shared",
    )(cnt, nblk, bst, dest3, hp, hb, wgs, wus, wds)


def _rope_tables(n_pos):
    half = ROT_DIM // 2
    inv_freq = ROPE_THETA ** (-jnp.arange(half, dtype=F32) / half)
    ang = jnp.arange(n_pos).astype(F32)[:, None] * inv_freq[None, :]
    cos, sin = jnp.cos(ang), jnp.sin(ang)
    c = jnp.concatenate([cos, cos, jnp.ones((n_pos, HEAD_DIM - ROT_DIM), F32)], axis=1)
    s = jnp.concatenate([sin, sin, jnp.zeros((n_pos, HEAD_DIM - ROT_DIM), F32)], axis=1)
    return c, s


def kernel(x, meta_tokens, w_in, conv_w, conv_b, w_rg_a, b_rg_a, w_rg_x, b_rg_x, rg_lambda, attn_sinks, w_o_attn, w_o_lru, w_out, ln1_g, ln1_b, w_router, router_bias, w_gate_e, w_up_e, w_down_e, w_gate_s, w_up_s, w_down_s, ln2_g, ln2_b):
    batch, seq, _ = x.shape
    n_tok = batch * seq
    assert w_in.shape[0] == 1 and seq % 512 == 0 and meta_tokens.shape[0] == N_META
    x2 = x.reshape(n_tok, D_MODEL)
    row = lambda v: v.reshape(1, -1).astype(F32)

    wax = jnp.concatenate([w_rg_a[0], w_rg_x[0]], axis=-1).astype(BF16)
    c_t, s_t = _rope_tables(N_META + seq)
    tabs_meta = (c_t[:N_META], s_t[:N_META])
    tabs_real = (c_t[N_META:], s_t[N_META:])

    projm = _matmul(meta_tokens.astype(BF16), w_in[0], BF16, N_META, 1024)
    lru_args = (conv_w[0], row(conv_b[0]), wax, row(b_rg_a[0]), row(b_rg_x[0]), row(rg_lambda[0]))
    _, h_meta = _lru(projm, jnp.zeros((N_META, D_LRU), BF16), jnp.zeros((1, D_LRU), F32), *lru_args,
                     batch=1, seq=N_META, tc=N_META)

    proj = _matmul(x2.astype(BF16), w_in[0], BF16, 1024, 1024)
    o_attn = _attention(proj, projm, attn_sinks[0].astype(F32), tabs_real, tabs_meta, batch, seq)
    y_lru, _ = _lru(proj, projm[:, XR_COL:XR_COL + D_LRU], h_meta[0, 7:8], *lru_args, batch=batch, seq=seq, tc=256)
    z = _merge(o_attn, y_lru, w_o_attn[0].astype(BF16), w_o_lru[0].astype(BF16), proj)
    h1, h1b = _ln1(x2, z, w_out[0].astype(BF16), row(ln1_g[0]), row(ln1_b[0]))

    topi_t, topw_t = _router(h1b, w_router[0].T.astype(BF16), router_bias[0].reshape(N_EXPERTS, 1).astype(F32))
    dest3, info = _plan(topi_t)
    cnt, nblk, bst = info[0], info[1], info[2]
    n_rows = -(-(n_tok * TOP_K + N_EXPERTS * (EXPERT_BLOCK - 1)) // EXPERT_BLOCK) * EXPERT_BLOCK
    xs, shared = _dispatch(cnt, nblk, bst, dest3, h1, h1b,w_gate_s[0].astype(BF16), w_up_s[0].astype(BF16),
                           w_down_s[0].astype(BF16), n_rows)
    ys = _experts(nblk, bst, xs, w_gate_e[0], w_up_e[0], w_down_e[0])
    out = _combine(dest3, ys, topw_t.T, h1, shared, row(ln2_g[0]), row(ln2_b[0]))
    return out.reshape(batch, seq, D_MODEL)
```

```python
import functools

import jax
import jax.numpy as jnp
from jax import lax
from jax.experimental import pallas as pl
from jax.experimental.pallas import tpu as pltpu

F32 = jnp.float32
BF16 = jnp.bfloat16
U32 = jnp.uint32
I32 = jnp.int32

D_MODEL = 2048
N_META = 16
N_Q_HEADS = 16
N_KV_HEADS = 4
HEAD_DIM = 128
Q_GROUP = N_Q_HEADS // N_KV_HEADS
ROT_DIM = HEAD_DIM // 4
ROPE_THETA = 500000.0
ATTN_BLOCK = 128
D_LRU = D_MODEL
LRU_BLOCKS = 16
LRU_BW = D_LRU // LRU_BLOCKS
CONV_W = 4
RGLRU_C = 8.0
N_EXPERTS = 256
TOP_K = 8
N_GROUPS = 8
GROUP_SIZE = N_EXPERTS // N_GROUPS
TOPK_GROUPS = 4
D_EXPERT = 512
ROUTED_SCALE = 2.5
EXPERT_BLOCK = 128
EXPERT_RING = 4
DN_ALPHA = 2.0 ** 0.25
LN_EPS = 1e-5
NEG_INF = -1e30
KV_DIM = N_KV_HEADS * HEAD_DIM
Q_DIM = N_Q_HEADS * HEAD_DIM
K_COLBLK = Q_DIM // KV_DIM
V_COLBLK = K_COLBLK + 1
XR_COL = Q_DIM + 2 * KV_DIM
XG_COL = XR_COL + D_LRU
GA_COL = XG_COL + D_LRU
GL_COL = GA_COL + D_MODEL
LRU_HALF = D_LRU // 2
VMEM_LIMIT = 50 * 1024 * 1024


def _cparams(sem):
    return pltpu.CompilerParams(dimension_semantics=sem, vmem_limit_bytes=VMEM_LIMIT)


def _mm_kernel(a_ref, b_ref, o_ref, b_bf):
    @pl.when(pl.program_id(1) == 0)
    def _():
        b_bf[...] = b_ref[...].astype(BF16)

    o_ref[...] = jnp.dot(a_ref[...], b_bf[...], preferred_element_type=F32).astype(o_ref.dtype)


def _matmul(a, b, out_dtype, tm, tn):
    m, k = a.shape
    n = b.shape[1]
    return pl.pallas_call(
        _mm_kernel,
        out_shape=jax.ShapeDtypeStruct((m, n), out_dtype),
        grid=(n // tn, m // tm),
        in_specs=[pl.BlockSpec((tm, k), lambda j, i: (i, 0)),
                  pl.BlockSpec((k, tn), lambda j, i: (0, j))],
        out_specs=pl.BlockSpec((tm, tn), lambda j, i: (i, j)),
        scratch_shapes=[pltpu.VMEM((k, tn), BF16)],
        compiler_params=_cparams(("arbitrary", "arbitrary")),
        name="dense_matmul",
    )(a, b)


def _rope(x, rot, c, s):
    return x.astype(F32) * c + jnp.dot(x, rot, preferred_element_type=F32) * s


def _attn_kernel(sink_ref, q_ref, kc_ref, kp_ref, vc_ref, vp_ref, km_ref, vm_ref,
                 cq_ref, sq_ref, cp_ref, sp_ref, cm_ref, sm_ref, o_ref, *, tq):
    n = pl.program_id(1)
    nsub = tq // ATTN_BLOCK
    scale = HEAD_DIM ** -0.5
    cq, sq = cq_ref[...], sq_ref[...]
    half = ROT_DIM // 2
    rj = lax.broadcasted_iota(I32, (HEAD_DIM, HEAD_DIM), 0)
    ri = lax.broadcasted_iota(I32, (HEAD_DIM, HEAD_DIM), 1)
    rot = (jnp.where(jnp.logical_and(ri < half, rj == ri + half), -1.0, 0.0)
           + jnp.where(jnp.logical_and(jnp.logical_and(ri >= half, ri < ROT_DIM), rj == ri - half), 1.0, 0.0)
           ).astype(BF16)
    rows = Q_GROUP * ATTN_BLOCK
    r_idx = lax.broadcasted_iota(I32, (rows, ATTN_BLOCK), 0) % ATTN_BLOCK
    c_idx = lax.broadcasted_iota(I32, (rows, ATTN_BLOCK), 1)
    cur_ok = c_idx <= r_idx
    prev_ok = c_idx > r_idx
    first_prev_ok = (c_idx - r_idx) > jnp.where(n > 0, 0, ATTN_BLOCK)
    meta_ok = c_idx < N_META
    pad_rows = jnp.zeros((ATTN_BLOCK - N_META, HEAD_DIM), BF16)
    ones_blk = jnp.ones((3 * ATTN_BLOCK, HEAD_DIM), BF16)
    dn = (((1,), (1,)), ((), ()))
    for g in range(N_KV_HEADS):
        hs = slice(g * HEAD_DIM, (g + 1) * HEAD_DIM)
        k_g = _rope(kc_ref[:, hs], rot, cq, sq).astype(BF16)
        kp_g = _rope(kp_ref[:, hs], rot, cp_ref[...], sp_ref[...]).astype(BF16)
        km_g = jnp.concatenate([_rope(km_ref[:, hs], rot, cm_ref[...], sm_ref[...]).astype(BF16), pad_rows], axis=0)
        v_g = vc_ref[:, hs]
        vp_g = vp_ref[:, hs]
        vm_g = jnp.concatenate([vm_ref[:, hs], pad_rows], axis=0)
        sink = jnp.concatenate(
            [jnp.full((ATTN_BLOCK, 1), sink_ref[g * Q_GROUP + j], F32) for j in range(Q_GROUP)], axis=0)
        for sub in range(nsub):
            rs = slice(sub * ATTN_BLOCK, (sub + 1) * ATTN_BLOCK)
            q_st = jnp.concatenate(
                [_rope(q_ref[rs, (g * Q_GROUP + j) * HEAD_DIM:(g * Q_GROUP + j + 1) * HEAD_DIM], rot,
                       cq[rs], sq[rs]) for j in range(Q_GROUP)], axis=0).astype(BF16)
            if sub == 0:
                k_all = jnp.concatenate([kp_g, k_g[rs], km_g], axis=0)
                v_all = jnp.concatenate([vp_g, v_g[rs], vm_g], axis=0)
                p_ok = first_prev_ok
            else:
                both = slice((sub - 1) * ATTN_BLOCK, (sub + 1) * ATTN_BLOCK)
                k_all = jnp.concatenate([k_g[both], km_g], axis=0)
                v_all = jnp.concatenate([v_g[both], vm_g], axis=0)
                p_ok = prev_ok
            sc = lax.dot_general(q_st, k_all, dn, preferred_element_type=F32) * scale
            s_p = jnp.where(p_ok, sc[:, :ATTN_BLOCK], NEG_INF)
            s_c = jnp.where(cur_ok, sc[:, ATTN_BLOCK:2 * ATTN_BLOCK], NEG_INF)
            s_m = jnp.where(meta_ok, sc[:, 2 * ATTN_BLOCK:], NEG_INF)
            m = jnp.maximum(jnp.max(jnp.maximum(jnp.maximum(s_p, s_c), s_m), axis=1, keepdims=True), sink)
            p = jnp.concatenate([jnp.exp(s_p - m), jnp.exp(s_c - m), jnp.exp(s_m - m)], axis=1).astype(BF16)
            od = jnp.dot(p, jnp.concatenate([v_all, ones_blk], axis=1), preferred_element_type=F32)
            o = od[:, :HEAD_DIM] / (od[:, HEAD_DIM:] + jnp.exp(sink - m))
            for j in range(Q_GROUP):
                h = g * Q_GROUP + j
                o_ref[rs, h * HEAD_DIM:(h + 1) * HEAD_DIM] = o[j * ATTN_BLOCK:(j + 1) * ATTN_BLOCK].astype(BF16)


def _attention(proj, projm, sinks, tabs_real, tabs_meta, batch, seq, tq=512):
    n_tok = batch * seq
    nq = seq // tq
    sub_per = tq // ATTN_BLOCK

    def prev_blk(b, n):
        return jnp.maximum(b * (seq // ATTN_BLOCK) + n * sub_per - 1, 0)

    tab_spec = pl.BlockSpec((tq, HEAD_DIM), lambda b, n, s: (n, 0))
    tabp_spec = pl.BlockSpec((ATTN_BLOCK, HEAD_DIM), lambda b, n, s: (jnp.maximum(n * sub_per - 1, 0), 0))
    tabm_spec = pl.BlockSpec((N_META, HEAD_DIM), lambda b, n, s: (0, 0))
    in_specs = [
        pl.BlockSpec((tq, D_MODEL), lambda b, n, s: (b * nq + n, 0)),
        pl.BlockSpec((tq, KV_DIM), lambda b, n, s: (b * nq + n, K_COLBLK)),
        pl.BlockSpec((ATTN_BLOCK, KV_DIM), lambda b, n, s: (prev_blk(b, n), K_COLBLK)),
        pl.BlockSpec((tq, KV_DIM), lambda b, n, s: (b * nq + n, V_COLBLK)),
        pl.BlockSpec((ATTN_BLOCK, KV_DIM), lambda b, n, s: (prev_blk(b, n), V_COLBLK)),
        pl.BlockSpec((N_META, KV_DIM), lambda b, n, s: (0, K_COLBLK)),
        pl.BlockSpec((N_META, KV_DIM), lambda b, n, s: (0, V_COLBLK)),
        tab_spec, tab_spec, tabp_spec, tabp_spec, tabm_spec, tabm_spec,
    ]
    grid_spec = pltpu.PrefetchScalarGridSpec(
        num_scalar_prefetch=1, grid=(batch, nq), in_specs=in_specs,
        out_specs=pl.BlockSpec((tq, D_MODEL), lambda b, n, s: (b * nq + n, 0)))
    return pl.pallas_call(
        functools.partial(_attn_kernel, tq=tq),
        out_shape=jax.ShapeDtypeStruct((n_tok, D_MODEL), BF16),
        grid_spec=grid_spec,
        compiler_params=_cparams(("parallel", "parallel")),
        name="swa_attention",
    )(sinks, proj, proj, proj, proj, proj, projm, projm,
      tabs_real[0], tabs_real[1], tabs_real[0], tabs_real[1], tabs_meta[0], tabs_meta[1])


def _softplus(z):
    return jnp.maximum(z, 0.0) + jnp.log1p(jnp.exp(-jnp.abs(z)))


def _lru_kernel(xr_ref, prev_ref, hist0_ref, xg_ref, cw_ref, cb_ref, wax_ref, ba_ref, bx_ref, lam_ref, h0_ref,
                y_ref, hl_ref, a_s, u_s, hcar, *, tc):
    c = pl.program_id(2)
    width = a_s.shape[1]
    blocks = width // LRU_BW

    @pl.when(c == 0)
    def _():
        hcar[...] = jnp.broadcast_to(h0_ref[...], hcar.shape)

    first = c == 0
    for nb in range(blocks):
        cs = slice(nb * LRU_BW, (nb + 1) * LRU_BW)
        hist = jnp.where(first, hist0_ref[:, cs], prev_ref[:, cs]).astype(F32)
        x = xr_ref[:, cs].astype(F32)
        ext = jnp.concatenate([hist, x], axis=0)
        cw = cw_ref[:, cs]
        y = cb_ref[:, cs] + x * cw[CONV_W - 1:CONV_W]
        for d in range(1, CONV_W):
            y = y + pltpu.roll(ext, d, 0)[N_META:] * cw[CONV_W - 1 - d:CONV_W - d]
        gates = jnp.dot(y.astype(BF16), wax_ref[nb], preferred_element_type=F32)
        r = jax.nn.sigmoid(gates[:, :LRU_BW] + ba_ref[:, cs])
        ig = jax.nn.sigmoid(gates[:, LRU_BW:] + bx_ref[:, cs])
        log_a = (-RGLRU_C) * r * _softplus(-lam_ref[:, cs])
        a = jnp.exp(log_a)
        a_s[:, cs] = a
        u_s[:, cs] = jnp.sqrt(-jnp.tanh(log_a) * (a * a + 1.0)) * (ig * y)

    row = lax.broadcasted_iota(I32, (8, width), 0)

    def body(i, h):
        sl = pl.ds(pl.multiple_of(i * 8, 8), 8)
        a = a_s[sl, :]
        u = u_s[sl, :]
        for d in (1, 2, 4):
            ok = row >= d
            u = jnp.where(ok, a * pltpu.roll(u, d, 0) + u, u)
            a = jnp.where(ok, a * pltpu.roll(a, d, 0), a)
        hs = a * h + u
        u_s[sl, :] = hs
        return jnp.broadcast_to(hs[7:8, :], (8, width))

    h_fin = lax.fori_loop(0, tc // 8, body, hcar[...])
    hcar[...] = h_fin
    hl_ref[0] = h_fin
    for nb in range(blocks):
        cs = slice(nb * LRU_BW, (nb + 1) * LRU_BW)
        y_ref[:, cs] = (u_s[:, cs] * jax.nn.gelu(xg_ref[:, cs].astype(F32))).astype(BF16)


def _lru(proj, hist0, h0, cw, cb, wax, ba, bx, lam, batch, seq, tc):
    nch = seq // tc
    per16 = tc // N_META

    xr_blk = XR_COL // LRU_HALF
    xg_blk = XG_COL // LRU_HALF

    def prev_idx(b, h, c):
        return (jnp.maximum(b * (seq // N_META) + c * per16 - 1, 0), xr_blk + h)

    vec = pl.BlockSpec((1, LRU_HALF), lambda b, h, c: (0, h))
    in_specs = [
        pl.BlockSpec((tc, LRU_HALF), lambda b, h, c: (b * nch + c, xr_blk + h)),
        pl.BlockSpec((N_META, LRU_HALF), prev_idx),
        pl.BlockSpec((N_META, LRU_HALF), lambda b, h, c: (0, h)),
        pl.BlockSpec((tc, LRU_HALF), lambda b, h, c: (b * nch + c, xg_blk + h)),
        pl.BlockSpec((CONV_W, LRU_HALF), lambda b, h, c: (0, h)),
        vec,
        pl.BlockSpec((LRU_BLOCKS // 2, LRU_BW, 2 * LRU_BW), lambda b, h, c: (h, 0, 0)),
        vec, vec, vec, vec,
    ]
    return pl.pallas_call(
        functools.partial(_lru_kernel, tc=tc),
        out_shape=(jax.ShapeDtypeStruct((batch * seq, D_LRU), BF16),
                   jax.ShapeDtypeStruct((batch, 8, D_LRU), F32)),
        grid=(batch, 2, nch),
        in_specs=in_specs,
        out_specs=(pl.BlockSpec((tc, LRU_HALF), lambda b, h, c: (b * nch + c, h)),
                   pl.BlockSpec((1, 8, LRU_HALF), lambda b, h, c: (b, 0, h))),
        scratch_shapes=[pltpu.VMEM((tc, LRU_HALF), F32), pltpu.VMEM((tc, LRU_HALF), F32),
                        pltpu.VMEM((8, LRU_HALF), F32)],
        compiler_params=_cparams(("arbitrary", "arbitrary", "arbitrary")),
        name="conv_rglru",
    )(proj, proj, hist0, proj, cw, cb, wax, ba, bx, lam, h0)


def _merge_kernel(o_ref, y_ref, wa_ref, wl_ref, ga_ref, gl_ref, z_ref):
    ya = jnp.dot(o_ref[...], wa_ref[...], preferred_element_type=F32)
    yl = jnp.dot(y_ref[...], wl_ref[...], preferred_element_type=F32)
    z = jax.nn.sigmoid(ga_ref[...].astype(F32)) * ya + jax.nn.sigmoid(gl_ref[...].astype(F32)) * yl
    z_ref[...] = z.astype(BF16)


def _merge(o_attn, y_lru, wa, wl, proj, tm=512, tn=1024):
    n_tok = o_attn.shape[0]
    ga_blk = GA_COL // tn
    gl_blk = GL_COL // tn
    return pl.pallas_call(
        _merge_kernel,
        out_shape=jax.ShapeDtypeStruct((n_tok, D_MODEL), BF16),
        grid=(D_MODEL // tn, n_tok // tm),
        in_specs=[pl.BlockSpec((tm, D_MODEL), lambda j, i: (i, 0)),
                  pl.BlockSpec((tm, D_MODEL), lambda j, i: (i, 0)),
                  pl.BlockSpec((D_MODEL, tn), lambda j, i: (0, j)),
                  pl.BlockSpec((D_MODEL, tn), lambda j, i: (0, j)),
                  pl.BlockSpec((tm, tn), lambda j, i: (i, ga_blk + j)),
                  pl.BlockSpec((tm, tn), lambda j, i: (i, gl_blk + j))],
        out_specs=pl.BlockSpec((tm, tn), lambda j, i: (i, j)),
        compiler_params=_cparams(("parallel", "parallel")),
        name="gated_merge",
    )(o_attn, y_lru, wa, wl, proj, proj)


def _layernorm(v, g, b):
    mu = jnp.mean(v, axis=-1, keepdims=True)
    dv = v - mu
    var = jnp.mean(dv * dv, axis=-1, keepdims=True)
    return dv * lax.rsqrt(var + LN_EPS) * g + b


def _ln1_kernel(x_ref, z_ref, w_ref, g_ref, b_ref, h_ref, hb_ref):
    mix = jnp.dot(z_ref[...], w_ref[...], preferred_element_type=F32)
    h = _layernorm(DN_ALPHA * x_ref[...] + mix, g_ref[...], b_ref[...])
    h_ref[...] = h
    hb_ref[...] = h.astype(BF16)


def _ln1(x2, z, w_out, g, b, tm=256):
    n_tok = x2.shape[0]
    row = lambda i: (i, 0)
    fixed = lambda i: (0, 0)
    return pl.pallas_call(
        _ln1_kernel,
        out_shape=(jax.ShapeDtypeStruct((n_tok, D_MODEL), F32),
                   jax.ShapeDtypeStruct((n_tok, D_MODEL), BF16)),
        grid=(n_tok // tm,),
        in_specs=[pl.BlockSpec((tm, D_MODEL), row), pl.BlockSpec((tm, D_MODEL), row),
                  pl.BlockSpec((D_MODEL, D_MODEL), fixed),
                  pl.BlockSpec((1, D_MODEL), fixed), pl.BlockSpec((1, D_MODEL), fixed)],
        out_specs=(pl.BlockSpec((tm, D_MODEL), row), pl.BlockSpec((tm, D_MODEL), row)),
        compiler_params=_cparams(("parallel",)),
        name="outproj_ln1",
    )(x2, z, w_out, g, b)


def _router_kernel(h_ref, w_ref, bias_ref, ti_ref, tw_ref, *, tm):
    logits = lax.dot_general(w_ref[...], h_ref[...], (((1,), (1,)), ((), ())), preferred_element_type=F32)
    scores = jax.nn.sigmoid(logits)
    sel = scores + bias_ref[...]
    neg = -jnp.inf
    e_iota = lax.broadcasted_iota(I32, (N_EXPERTS, tm), 0).astype(F32)
    g_iota32 = lax.broadcasted_iota(I32, (GROUP_SIZE, tm), 0).astype(F32)
    gs_rows = []
    for g in range(N_GROUPS):
        blk = sel[g * GROUP_SIZE:(g + 1) * GROUP_SIZE]
        m1 = jnp.max(blk, axis=0, keepdims=True)
        i1 = jnp.min(jnp.where(blk == m1, g_iota32, float(GROUP_SIZE)), axis=0, keepdims=True)
        m2 = jnp.max(jnp.where(g_iota32 == i1, neg, blk), axis=0, keepdims=True)
        gs_rows.append(m1 + m2)
    gs = jnp.concatenate(gs_rows, axis=0)
    g_iota = lax.broadcasted_iota(I32, (N_GROUPS, tm), 0).astype(F32)
    gsel = jnp.zeros((N_GROUPS, tm), F32)
    for _ in range(TOPK_GROUPS):
        m = jnp.max(gs, axis=0, keepdims=True)
        idx = jnp.min(jnp.where(gs == m, g_iota, float(N_GROUPS)), axis=0, keepdims=True)
        hit = g_iota == idx
        gsel = jnp.where(hit, 1.0, gsel)
        gs = jnp.where(hit, neg, gs)
    masked = jnp.concatenate(
        [jnp.where(gsel[g:g + 1] > 0.0, sel[g * GROUP_SIZE:(g + 1) * GROUP_SIZE], neg) for g in range(N_GROUPS)],
        axis=0)
    tis, tws = [], []
    for _ in range(TOP_K):
        m = jnp.max(masked, axis=0, keepdims=True)
        idx = jnp.min(jnp.where(masked == m, e_iota, float(N_EXPERTS)), axis=0, keepdims=True)
        hit = e_iota == idx
        tis.append(idx)
        tws.append(jnp.sum(jnp.where(hit, scores, 0.0), axis=0, keepdims=True))
        masked = jnp.where(hit, neg, masked)
    tw = jnp.concatenate(tws, axis=0)
    ti_ref[...] = jnp.concatenate(tis, axis=0).astype(I32)
    tw_ref[...] = tw / jnp.sum(tw, axis=0, keepdims=True) * ROUTED_SCALE


def _router(hb, wr_t, bias, tm=512):
    n_tok = hb.shape[0]
    return pl.pallas_call(
        functools.partial(_router_kernel, tm=tm),
        out_shape=(jax.ShapeDtypeStruct((TOP_K, n_tok), I32), jax.ShapeDtypeStruct((TOP_K, n_tok), F32)),
        grid=(n_tok // tm,),
        in_specs=[pl.BlockSpec((tm, D_MODEL), lambda i: (i, 0)),
                  pl.BlockSpec((N_EXPERTS, D_MODEL), lambda i: (0, 0)),
                  pl.BlockSpec((N_EXPERTS, 1), lambda i: (0, 0))],
        out_specs=(pl.BlockSpec((TOP_K, tm), lambda i: (0, i)), pl.BlockSpec((TOP_K, tm), lambda i: (0, i))),
        compiler_params=_cparams(("parallel",)),
        name="router_topk",
    )(hb, wr_t, bias)


def _expert_kernel(nblk_ref, bst_ref, xs_ref, wg_ref, wu_ref, wd_ref, ys_ref,
                   xbuf, obuf, isem, osem, wgb, wub, wdb):
    e = pl.program_id(0)
    nb = nblk_ref[e]
    b0 = bst_ref[e]
    nused = bst_ref[N_EXPERTS - 1] + nblk_ref[N_EXPERTS - 1]
    ahead = EXPERT_RING - 1

    def rows(g):
        return pl.ds(pl.multiple_of(g * EXPERT_BLOCK, EXPERT_BLOCK), EXPERT_BLOCK)

    def in_copy(g):
        slot = g & (EXPERT_RING - 1)
        return pltpu.make_async_copy(xs_ref.at[rows(g), :], xbuf.at[slot], isem.at[slot])

    def out_copy(g):
        slot = g & (EXPERT_RING - 1)
        return pltpu.make_async_copy(obuf.at[slot], ys_ref.at[rows(g), :], osem.at[slot])

    @pl.when(nb > 0)
    def _():
        @pl.when(b0 == 0)
        def _():
            for q in range(ahead):
                @pl.when(q < nused)
                def _(q=q):
                    in_copy(q).start(priority=1)

        wgb[...] = wg_ref[0].astype(BF16)
        wub[...] = wu_ref[0].astype(BF16)
        wdb[...] = wd_ref[0].astype(BF16)

        def body(j, carry):
            g = b0 + j
            slot = g & (EXPERT_RING - 1)

            @pl.when(g + ahead < nused)
            def _():
                in_copy(g + ahead).start(priority=1)

            in_copy(g).wait()
            x = xbuf[slot].astype(BF16)
            hid = jax.nn.silu(jnp.dot(x, wgb[...], preferred_element_type=F32)) * jnp.dot(
                x, wub[...], preferred_element_type=F32)
            out = jnp.dot(hid.astype(BF16), wdb[...], preferred_element_type=F32)

            @pl.when(g >= EXPERT_RING)
            def _():
                out_copy(g - EXPERT_RING).wait()

            obuf[slot] = out
            out_copy(g).start(priority=1)
            return carry

        lax.fori_loop(0, nb, body, 0)

    @pl.when(e == N_EXPERTS - 1)
    def _():
        for q in range(EXPERT_RING, 0, -1):
            @pl.when(nused >= q)
            def _(q=q):
                out_copy(nused - q).wait()


def _experts(nblk, bst, xs, wg, wu, wd):
    grid_spec = pltpu.PrefetchScalarGridSpec(
        num_scalar_prefetch=2, grid=(N_EXPERTS,),
        in_specs=[pl.BlockSpec(memory_space=pl.ANY),
                  pl.BlockSpec((1, D_MODEL, D_EXPERT), lambda e, nb, bs: (e, 0, 0)),
                  pl.BlockSpec((1, D_MODEL, D_EXPERT), lambda e, nb, bs: (e, 0, 0)),
                  pl.BlockSpec((1, D_EXPERT, D_MODEL), lambda e, nb, bs: (e, 0, 0))],
        out_specs=pl.BlockSpec(memory_space=pl.ANY),
        scratch_shapes=[pltpu.VMEM((EXPERT_RING, EXPERT_BLOCK, D_MODEL), F32),
                        pltpu.VMEM((EXPERT_RING, EXPERT_BLOCK, D_MODEL), F32),
                        pltpu.SemaphoreType.DMA((EXPERT_RING,)), pltpu.SemaphoreType.DMA((EXPERT_RING,)),
                        pltpu.VMEM((D_MODEL, D_EXPERT), BF16), pltpu.VMEM((D_MODEL, D_EXPERT), BF16),
                        pltpu.VMEM((D_EXPERT, D_MODEL), BF16)])
    return pl.pallas_call(
        _expert_kernel,
        out_shape=jax.ShapeDtypeStruct(xs.shape, F32),
        grid_spec=grid_spec,
        input_output_aliases={2: 0},
        compiler_params=_cparams(("arbitrary",)),
        name="routed_experts",
    )(nblk, bst, xs, wg, wu, wd)


COMBINE_CHUNK = 16


def _combine_kernel(dc_ref, dn_ref, ys_ref, tw_ref, h_ref, sh_ref, g_ref, b_ref, o_ref, gbuf, sem, *, tm):
    i = pl.program_id(0)
    nsteps = pl.num_programs(0)
    nchunks = tm // COMBINE_CHUNK

    def issue_tile(d_ref, s):
        def body(t, carry):
            for k in range(TOP_K):
                pltpu.make_async_copy(ys_ref.at[pl.ds(d_ref[0, k, t], 1), :],
                                      gbuf.at[s, k, pl.ds(t, 1), :], sem.at[s]).start(priority=k % 2)
            return carry
        lax.fori_loop(0, tm, body, 0, unroll=2)

    def wait_slabs(s):
        for k in range(TOP_K):
            pltpu.make_async_copy(ys_ref.at[pl.ds(0, tm), :], gbuf.at[s, k], sem.at[s]).wait()

    @pl.when(i == 0)
    def _():
        issue_tile(dc_ref, 0)

    g = g_ref[...]
    b = b_ref[...]

    def reduce_tile(s):
        def chunk(c, carry):
            rs = pl.ds(pl.multiple_of(c * COMBINE_CHUNK, COMBINE_CHUNK), COMBINE_CHUNK)
            tw = tw_ref[rs, :]
            routed = tw[:, 0:1] * gbuf[s, 0, rs, :]
            for k in range(1, TOP_K):
                routed = routed + tw[:, k:k + 1] * gbuf[s, k, rs, :]
            v = DN_ALPHA * h_ref[rs, :] + (routed + sh_ref[rs, :].astype(F32))
            o_ref[rs, :] = _layernorm(v, g, b)
            return carry
        lax.fori_loop(0, nchunks, chunk, 0)

    for s in (0, 1):
        @pl.when((i & 1) == s)
        def _(s=s):
            @pl.when(i + 1 < nsteps)
            def _():
                issue_tile(dn_ref, 1 - s)

            wait_slabs(s)
            reduce_tile(s)


def _combine(dest3, ys, topw, h1, shared, g, b):
    n_tok = h1.shape[0]
    nsteps, _, tm = dest3.shape
    row = lambda i: (i, 0)
    fixed = lambda i: (0, 0)
    return pl.pallas_call(
        functools.partial(_combine_kernel, tm=tm),
        out_shape=jax.ShapeDtypeStruct((n_tok, D_MODEL), F32),
        grid=(nsteps,),
        in_specs=[pl.BlockSpec((1, TOP_K, tm), lambda i: (i, 0, 0), memory_space=pltpu.SMEM),
                  pl.BlockSpec((1, TOP_K, tm), lambda i: (jnp.minimum(i + 1, nsteps - 1), 0, 0),
                               memory_space=pltpu.SMEM),
                  pl.BlockSpec(memory_space=pl.ANY),
                  pl.BlockSpec((tm, TOP_K), row),
                  pl.BlockSpec((tm, D_MODEL), row), pl.BlockSpec((tm, D_MODEL), row),
                  pl.BlockSpec((1, D_MODEL), fixed), pl.BlockSpec((1, D_MODEL), fixed)],
        out_specs=pl.BlockSpec((tm, D_MODEL), row),
        scratch_shapes=[pltpu.VMEM((2, TOP_K, tm, D_MODEL), F32), pltpu.SemaphoreType.DMA((2,))],
        compiler_params=_cparams(("arbitrary",)),
        name="combine_ln2",
    )(dest3, dest3, ys, topw, h1, shared, g, b)


PLAN_TM = 128
LANES = 128


def _small_int_halves(v):
    hi = jnp.floor(v * (1.0 / 256.0))
    return hi.astype(BF16), (v - hi * 256.0).astype(BF16)


def _plan_kernel(ti_ref, dest_ref, info_ref, rank_s, cntc_s, cntr_s, pst_s, *, tm):
    p = pl.program_id(0)
    i = pl.program_id(1)
    ti = ti_ref[...]
    e_iota = lax.broadcasted_iota(I32, (N_EXPERTS, tm), 0)
    reps = tm // LANES

    @pl.when(jnp.logical_and(p == 0, i == 0))
    def _():
        cntc_s[...] = jnp.zeros(cntc_s.shape, F32)
        cntr_s[...] = jnp.zeros(cntr_s.shape, F32)

    @pl.when(p == 0)
    def _():
        hits = [e_iota == ti[k:k + 1] for k in range(TOP_K)]
        m = jnp.where(hits[0], 1.0, 0.0)
        for k in range(1, TOP_K):
            m = m + jnp.where(hits[k], 1.0, 0.0)
        mb = m.astype(BF16)
        earlier = lax.broadcasted_iota(I32, (tm, tm), 0) < lax.broadcasted_iota(I32, (tm, tm), 1)
        pfx = jnp.dot(mb, jnp.where(earlier, 1.0, 0.0).astype(BF16), preferred_element_type=F32)
        val = pfx + jnp.concatenate([cntc_s[...]] * reps, axis=1)
        rank_s[i] = jnp.concatenate(
            [jnp.sum(jnp.where(hits[k], val, 0.0), axis=0, keepdims=True) for k in range(TOP_K)], axis=0)
        cntc_s[...] += jnp.dot(mb, jnp.ones((tm, LANES), BF16), preferred_element_type=F32)
        cntr_s[...] += lax.dot_general(jnp.ones((8, tm), BF16), mb, (((1,), (1,)), ((), ())),
                                       preferred_element_type=F32)

    @pl.when(jnp.logical_and(p == 1, i == 0))
    def _():
        def n_blocks_of(cnt):
            return jnp.right_shift(cnt.astype(I32) + (EXPERT_BLOCK - 1), EXPERT_BLOCK.bit_length() - 1).astype(F32)

        ee0 = lax.broadcasted_iota(I32, (N_EXPERTS, N_EXPERTS), 0)
        ee1 = lax.broadcasted_iota(I32, (N_EXPERTS, N_EXPERTS), 1)
        hi, lo = _small_int_halves(n_blocks_of(cntc_s[...]))
        lower = jnp.where(ee1 < ee0, 1.0, 0.0).astype(BF16)
        bst_c = 256.0 * jnp.dot(lower, hi, preferred_element_type=F32) + jnp.dot(lower, lo, preferred_element_type=F32)
        pst_s[...] = bst_c * float(EXPERT_BLOCK)
        cnt_r = cntr_s[...]
        nb_r = n_blocks_of(cnt_r)
        hi, lo = _small_int_halves(nb_r)
        upper = jnp.where(ee0 < ee1, 1.0, 0.0).astype(BF16)
        bst_r = 256.0 * jnp.dot(hi, upper, preferred_element_type=F32) + jnp.dot(lo, upper, preferred_element_type=F32)
        sel = lax.broadcasted_iota(I32, (8, N_EXPERTS), 0)
        info = jnp.where(sel == 0, cnt_r, jnp.where(sel == 1, nb_r, jnp.where(sel == 2, bst_r, 0.0)))
        info_ref[...] = info.astype(I32)

    @pl.when(p == 1)
    def _():
        pst = jnp.concatenate([pst_s[...]] * reps, axis=1)
        off = jnp.concatenate(
            [jnp.sum(jnp.where(e_iota == ti[k:k + 1], pst, 0.0), axis=0, keepdims=True) for k in range(TOP_K)], axis=0)
        dest_ref[0] = (rank_s[i] + off).astype(I32)


def _plan(topi_t, tm=PLAN_TM):
    n_tok = topi_t.shape[1]
    nt = n_tok // tm
    return pl.pallas_call(
        functools.partial(_plan_kernel, tm=tm),
        out_shape=(jax.ShapeDtypeStruct((nt, TOP_K, tm), I32), jax.ShapeDtypeStruct((8, N_EXPERTS), I32)),
        grid=(2, nt),
        in_specs=[pl.BlockSpec((TOP_K, tm), lambda p, i: (0, i))],
        out_specs=(pl.BlockSpec((1, TOP_K, tm), lambda p, i: (i * p, 0, 0)),
                   pl.BlockSpec((8, N_EXPERTS), lambda p, i: (0, 0))),
        scratch_shapes=[pltpu.VMEM((nt, TOP_K, tm), F32), pltpu.VMEM((N_EXPERTS, LANES), F32),
                        pltpu.VMEM((8, N_EXPERTS), F32), pltpu.VMEM((N_EXPERTS, LANES), F32)],
        compiler_params=_cparams(("arbitrary", "arbitrary")),
        name="expert_row_plan",
    )(topi_t)


def _dispatch_kernel(cnt_ref, nblk_ref, bst_ref, dest_ref, hp_ref, hb_ref, wg_ref, wu_ref, wd_ref, xs_ref, sh_ref,
                     zrow, sem, zsem, tsem, *, tm, epg, tpg, n_blocks):
    i = pl.program_id(0)

    @pl.when(i == 0)
    def _():
        zrow[...] = jnp.zeros(zrow.shape, F32)

    nused = bst_ref[N_EXPERTS - 1] + nblk_ref[N_EXPERTS - 1]
    t_lo = jnp.minimum(nused + i * tpg, n_blocks)
    t_hi = jnp.minimum(nused + (i + 1) * tpg, n_blocks)

    def tail_copy(blk):
        return pltpu.make_async_copy(
            zrow, xs_ref.at[pl.ds(pl.multiple_of(blk * EXPERT_BLOCK, EXPERT_BLOCK), EXPERT_BLOCK), :], tsem)

    def tfill(blk, carry):
        tail_copy(blk).start()
        return carry

    def twait(blk, carry):
        tail_copy(blk).wait()
        return carry

    lax.fori_loop(t_lo, t_hi, tfill, 0)

    def scatter(t, carry):
        for k in range(TOP_K):
            pltpu.make_async_copy(hp_ref.at[pl.ds(t, 1), :], xs_ref.at[pl.ds(dest_ref[0, k, t], 1), :],
                                  sem).start(priority=k % 2)
        return carry

    lax.fori_loop(0, tm, scatter, 0, unroll=2)

    def zfill(r, carry):
        pltpu.make_async_copy(zrow.at[pl.ds(0, 1), :], xs_ref.at[pl.ds(r, 1), :], zsem).start()
        return carry

    def zwait(r, carry):
        pltpu.make_async_copy(zrow.at[pl.ds(0, 1), :], xs_ref.at[pl.ds(0, 1), :], zsem).wait()
        return carry

    bounds = []
    for q in range(epg):
        e = i * epg + q
        lo = bst_ref[e] * EXPERT_BLOCK + cnt_ref[e]
        hi = (bst_ref[e] + nblk_ref[e]) * EXPERT_BLOCK
        bounds.append((lo, hi))
        lax.fori_loop(lo, hi, zfill, 0)

    hb = hb_ref[...]
    hid = jax.nn.silu(jnp.dot(hb, wg_ref[...], preferred_element_type=F32)) * jnp.dot(
        hb, wu_ref[...], preferred_element_type=F32)
    sh_ref[...] = jnp.dot(hid.astype(BF16), wd_ref[...], preferred_element_type=F32).astype(BF16)

    for k in range(TOP_K):
        pltpu.make_async_copy(hp_ref, xs_ref.at[pl.ds(0, tm), :], sem).wait()
    for lo, hi in bounds:
        lax.fori_loop(lo, hi, zwait, 0)
    lax.fori_loop(t_lo, t_hi, twait, 0)


def _dispatch(cnt, nblk, bst, dest3, hp, hb, wgs, wus, wds, n_rows):
    nsteps, _, tm = dest3.shape
    n_tok = hp.shape[0]
    n_blocks = n_rows // EXPERT_BLOCK
    assert N_EXPERTS % nsteps == 0
    row = lambda i, *_: (i, 0)
    fixed = lambda i, *_: (0, 0)
    grid_spec = pltpu.PrefetchScalarGridSpec(
        num_scalar_prefetch=3, grid=(nsteps,),
        in_specs=[pl.BlockSpec((1, TOP_K, tm), lambda i, *_: (i, 0, 0), memory_space=pltpu.SMEM),
                  pl.BlockSpec((tm, D_MODEL), row), pl.BlockSpec((tm, D_MODEL), row),
                  pl.BlockSpec((D_MODEL, D_EXPERT), fixed), pl.BlockSpec((D_MODEL, D_EXPERT), fixed),
                  pl.BlockSpec((D_EXPERT, D_MODEL), fixed)],
        out_specs=(pl.BlockSpec(memory_space=pl.ANY), pl.BlockSpec((tm, D_MODEL), row)),
        scratch_shapes=[pltpu.VMEM((EXPERT_BLOCK, D_MODEL), F32), pltpu.SemaphoreType.DMA(()),
                        pltpu.SemaphoreType.DMA(()), pltpu.SemaphoreType.DMA(())])
    return pl.pallas_call(
        functools.partial(_dispatch_kernel, tm=tm, epg=N_EXPERTS // nsteps, tpg=-(-n_blocks // nsteps),
                          n_blocks=n_blocks),
        out_shape=(jax.ShapeDtypeStruct((n_rows, D_MODEL), F32), jax.ShapeDtypeStruct((n_tok, D_MODEL), BF16)),
        grid_spec=grid_spec,
        compiler_params=_cparams(("arbitrary",)),
        name="dispatch_shared",
    )(cnt, nblk, bst, dest3, hp, hb, wgs, wus, wds)


def _rope_tables(n_pos):
    half = ROT_DIM // 2
    inv_freq = ROPE_THETA ** (-jnp.arange(half, dtype=F32) / half)
    ang = jnp.arange(n_pos).astype(F32)[:, None] * inv_freq[None, :]
    cos, sin = jnp.cos(ang), jnp.sin(ang)
    c = jnp.concatenate([cos, cos, jnp.ones((n_pos, HEAD_DIM - ROT_DIM), F32)], axis=1)
    s = jnp.concatenate([sin, sin, jnp.zeros((n_pos, HEAD_DIM - ROT_DIM), F32)], axis=1)
    return c, s


def kernel(x, meta_tokens, w_in, conv_w, conv_b, w_rg_a, b_rg_a, w_rg_x, b_rg_x, rg_lambda, attn_sinks, w_o_attn, w_o_lru, w_out, ln1_g, ln1_b, w_router, router_bias, w_gate_e, w_up_e, w_down_e, w_gate_s, w_up_s, w_down_s, ln2_g, ln2_b):
    batch, seq, _ = x.shape
    n_tok = batch * seq
    assert w_in.shape[0] == 1 and seq % 512 == 0 and meta_tokens.shape[0] == N_META
    x2 = x.reshape(n_tok, D_MODEL)
    row = lambda v: v.reshape(1, -1).astype(F32)

    wax = jnp.concatenate([w_rg_a[0], w_rg_x[0]], axis=-1).astype(BF16)
    c_t, s_t = _rope_tables(N_META + seq)
    tabs_meta = (c_t[:N_META], s_t[:N_META])
    tabs_real = (c_t[N_META:], s_t[N_META:])

    projm = _matmul(meta_tokens.astype(BF16), w_in[0], BF16, N_META, 1024)
    lru_args = (conv_w[0], row(conv_b[0]), wax, row(b_rg_a[0]), row(b_rg_x[0]), row(rg_lambda[0]))
    _, h_meta = _lru(projm, jnp.zeros((N_META, D_LRU), BF16), jnp.zeros((1, D_LRU), F32), *lru_args,
                     batch=1, seq=N_META, tc=N_META)

    proj = _matmul(x2.astype(BF16), w_in[0], BF16, 1024, 1024)
    o_attn = _attention(proj, projm, attn_sinks[0].astype(F32), tabs_real, tabs_meta, batch, seq)
    y_lru, _ = _lru(proj, projm[:, XR_COL:XR_COL + D_LRU], h_meta[0, 7:8], *lru_args, batch=batch, seq=seq, tc=256)
    z = _merge(o_attn, y_lru, w_o_attn[0].astype(BF16), w_o_lru[0].astype(BF16), proj)
    h1, h1b = _ln1(x2, z, w_out[0].astype(BF16), row(ln1_g[0]), row(ln1_b[0]))

    topi_t, topw_t = _router(h1b, w_router[0].T.astype(BF16), router_bias[0].reshape(N_EXPERTS, 1).astype(F32))
    dest3, info = _plan(topi_t)
    cnt, nblk, bst = info[0], info[1], info[2]
    n_rows = -(-(n_tok * TOP_K + N_EXPERTS * (EXPERT_BLOCK - 1)) // EXPERT_BLOCK) * EXPERT_BLOCK
    xs, shared = _dispatch(cnt, nblk, bst, dest3, h1, h1b,w_gate_s[0].astype(BF16), w_up_s[0].astype(BF16),
                           w_down_s[0].astype(BF16), n_rows)
    ys = _experts(nblk, bst, xs, w_gate_e[0], w_up_e[0], w_down_e[0])
    out = _combine(dest3, ys, topw_t.T, h1, shared, row(ln2_g[0]), row(ln2_b[0]))
    return out.reshape(batch, seq, D_MODEL)
```

```python
import functools

import jax
import jax.numpy as jnp
from jax import lax
from jax.experimental import pallas as pl
from jax.experimental.pallas import tpu as pltpu

F32 = jnp.float32
BF16 = jnp.bfloat16
U32 = jnp.uint32
I32 = jnp.int32

D_MODEL = 2048
N_META = 16
N_Q_HEADS = 16
N_KV_HEADS = 4
HEAD_DIM = 128
Q_GROUP = N_Q_HEADS // N_KV_HEADS
ROT_DIM = HEAD_DIM // 4
ROPE_THETA = 500000.0
ATTN_BLOCK = 128
D_LRU = D_MODEL
LRU_BLOCKS = 16
LRU_BW = D_LRU // LRU_BLOCKS
CONV_W = 4
RGLRU_C = 8.0
N_EXPERTS = 256
TOP_K = 8
N_GROUPS = 8
GROUP_SIZE = N_EXPERTS // N_GROUPS
TOPK_GROUPS = 4
D_EXPERT = 512
ROUTED_SCALE = 2.5
EXPERT_BLOCK = 128
EXPERT_RING = 4
DN_ALPHA = 2.0 ** 0.25
LN_EPS = 1e-5
NEG_INF = -1e30
KV_DIM = N_KV_HEADS * HEAD_DIM
Q_DIM = N_Q_HEADS * HEAD_DIM
K_COLBLK = Q_DIM // KV_DIM
V_COLBLK = K_COLBLK + 1
XR_COL = Q_DIM + 2 * KV_DIM
XG_COL = XR_COL + D_LRU
GA_COL = XG_COL + D_LRU
GL_COL = GA_COL + D_MODEL
LRU_HALF = D_LRU // 2
LRU_SUB = 256
VMEM_LIMIT = 50 * 1024 * 1024


def _cparams(sem, vmem_limit=VMEM_LIMIT):
    return pltpu.CompilerParams(dimension_semantics=sem, vmem_limit_bytes=vmem_limit)


def _mm_kernel(a_ref, b_ref, o_ref, b_bf):
    @pl.when(pl.program_id(1) == 0)
    def _():
        b_bf[...] = b_ref[...].astype(BF16)

    o_ref[...] = jnp.dot(a_ref[...].astype(BF16), b_bf[...], preferred_element_type=F32).astype(o_ref.dtype)


def _matmul(a, b, out_dtype, tm, tn):
    m, k = a.shape
    n = b.shape[1]
    return pl.pallas_call(
        _mm_kernel,
        out_shape=jax.ShapeDtypeStruct((m, n), out_dtype),
        grid=(n // tn, m // tm),
        in_specs=[pl.BlockSpec((tm, k), lambda j, i: (i, 0)),
                  pl.BlockSpec((k, tn), lambda j, i: (0, j))],
        out_specs=pl.BlockSpec((tm, tn), lambda j, i: (i, j)),
        scratch_shapes=[pltpu.VMEM((k, tn), BF16)],
        compiler_params=_cparams(("arbitrary", "arbitrary")),
        name="dense_matmul",
    )(a, b)


def _rope(x, rot, c, s):
    return x.astype(F32) * c + jnp.dot(x, rot, preferred_element_type=F32) * s


def _attn_kernel(sink_ref, q_ref, kc_ref, kp_ref, vc_ref, vp_ref, km_ref, vm_ref,
                 cq_ref, sq_ref, cp_ref, sp_ref, cm_ref, sm_ref, o_ref, *, tq):
    n = pl.program_id(1)
    nsub = tq // ATTN_BLOCK
    scale = HEAD_DIM ** -0.5
    cq, sq = cq_ref[...], sq_ref[...]
    half = ROT_DIM // 2
    rj = lax.broadcasted_iota(I32, (HEAD_DIM, HEAD_DIM), 0)
    ri = lax.broadcasted_iota(I32, (HEAD_DIM, HEAD_DIM), 1)
    rot = (jnp.where(jnp.logical_and(ri < half, rj == ri + half), -1.0, 0.0)
           + jnp.where(jnp.logical_and(jnp.logical_and(ri >= half, ri < ROT_DIM), rj == ri - half), 1.0, 0.0)
           ).astype(BF16)
    rows = Q_GROUP * ATTN_BLOCK
    r_idx = lax.broadcasted_iota(I32, (rows, ATTN_BLOCK), 0) % ATTN_BLOCK
    c_idx = lax.broadcasted_iota(I32, (rows, ATTN_BLOCK), 1)
    cur_ok = c_idx <= r_idx
    prev_ok = c_idx > r_idx
    first_prev_ok = (c_idx - r_idx) > jnp.where(n > 0, 0, ATTN_BLOCK)
    meta_ok = c_idx < N_META
    pad_rows = jnp.zeros((ATTN_BLOCK - N_META, HEAD_DIM), BF16)
    ones_blk = jnp.ones((3 * ATTN_BLOCK, HEAD_DIM), BF16)
    dn = (((1,), (1,)), ((), ()))
    for g in range(N_KV_HEADS):
        hs = slice(g * HEAD_DIM, (g + 1) * HEAD_DIM)
        k_g = _rope(kc_ref[:, hs], rot, cq, sq).astype(BF16)
        kp_g = _rope(kp_ref[:, hs], rot, cp_ref[...], sp_ref[...]).astype(BF16)
        km_g = jnp.concatenate([_rope(km_ref[:, hs], rot, cm_ref[...], sm_ref[...]).astype(BF16), pad_rows], axis=0)
        v_g = vc_ref[:, hs]
        vp_g = vp_ref[:, hs]
        vm_g = jnp.concatenate([vm_ref[:, hs], pad_rows], axis=0)
        sink = jnp.concatenate(
            [jnp.full((ATTN_BLOCK, 1), sink_ref[g * Q_GROUP + j], F32) for j in range(Q_GROUP)], axis=0)
        for sub in range(nsub):
            rs = slice(sub * ATTN_BLOCK, (sub + 1) * ATTN_BLOCK)
            q_st = jnp.concatenate(
                [_rope(q_ref[rs, (g * Q_GROUP + j) * HEAD_DIM:(g * Q_GROUP + j + 1) * HEAD_DIM], rot,
                       cq[rs], sq[rs]) for j in range(Q_GROUP)], axis=0).astype(BF16)
            if sub == 0:
                k_all = jnp.concatenate([kp_g, k_g[rs], km_g], axis=0)
                v_all = jnp.concatenate([vp_g, v_g[rs], vm_g], axis=0)
                p_ok = first_prev_ok
            else:
                both = slice((sub - 1) * ATTN_BLOCK, (sub + 1) * ATTN_BLOCK)
                k_all = jnp.concatenate([k_g[both], km_g], axis=0)
                v_all = jnp.concatenate([v_g[both], vm_g], axis=0)
                p_ok = prev_ok
            sc = lax.dot_general(q_st, k_all, dn, preferred_element_type=F32) * scale
            s_p = jnp.where(p_ok, sc[:, :ATTN_BLOCK], NEG_INF)
            s_c = jnp.where(cur_ok, sc[:, ATTN_BLOCK:2 * ATTN_BLOCK], NEG_INF)
            s_m = jnp.where(meta_ok, sc[:, 2 * ATTN_BLOCK:], NEG_INF)
            m = jnp.maximum(jnp.max(jnp.maximum(jnp.maximum(s_p, s_c), s_m), axis=1, keepdims=True), sink)
            p = jnp.concatenate([jnp.exp(s_p - m), jnp.exp(s_c - m), jnp.exp(s_m - m)], axis=1).astype(BF16)
            od = jnp.dot(p, jnp.concatenate([v_all, ones_blk], axis=1), preferred_element_type=F32)
            o = od[:, :HEAD_DIM] / (od[:, HEAD_DIM:] + jnp.exp(sink - m))
            for j in range(Q_GROUP):
                h = g * Q_GROUP + j
                o_ref[rs, h * HEAD_DIM:(h + 1) * HEAD_DIM] = o[j * ATTN_BLOCK:(j + 1) * ATTN_BLOCK].astype(BF16)


def _attention(proj, projm, sinks, tabs_real, tabs_meta, batch, seq, tq=512):
    n_tok = batch * seq
    nq = seq // tq
    sub_per = tq // ATTN_BLOCK

    def prev_blk(b, n):
        return jnp.maximum(b * (seq // ATTN_BLOCK) + n * sub_per - 1, 0)

    tab_spec = pl.BlockSpec((tq, HEAD_DIM), lambda b, n, s: (n, 0))
    tabp_spec = pl.BlockSpec((ATTN_BLOCK, HEAD_DIM), lambda b, n, s: (jnp.maximum(n * sub_per - 1, 0), 0))
    tabm_spec = pl.BlockSpec((N_META, HEAD_DIM), lambda b, n, s: (0, 0))
    in_specs = [
        pl.BlockSpec((tq, D_MODEL), lambda b, n, s: (b * nq + n, 0)),
        pl.BlockSpec((tq, KV_DIM), lambda b, n, s: (b * nq + n, K_COLBLK)),
        pl.BlockSpec((ATTN_BLOCK, KV_DIM), lambda b, n, s: (prev_blk(b, n), K_COLBLK)),
        pl.BlockSpec((tq, KV_DIM), lambda b, n, s: (b * nq + n, V_COLBLK)),
        pl.BlockSpec((ATTN_BLOCK, KV_DIM), lambda b, n, s: (prev_blk(b, n), V_COLBLK)),
        pl.BlockSpec((N_META, KV_DIM), lambda b, n, s: (0, K_COLBLK)),
        pl.BlockSpec((N_META, KV_DIM), lambda b, n, s: (0, V_COLBLK)),
        tab_spec, tab_spec, tabp_spec, tabp_spec, tabm_spec, tabm_spec,
    ]
    grid_spec = pltpu.PrefetchScalarGridSpec(
        num_scalar_prefetch=1, grid=(batch, nq), in_specs=in_specs,
        out_specs=pl.BlockSpec((tq, D_MODEL), lambda b, n, s: (b * nq + n, 0)))
    return pl.pallas_call(
        functools.partial(_attn_kernel, tq=tq),
        out_shape=jax.ShapeDtypeStruct((n_tok, D_MODEL), BF16),
        grid_spec=grid_spec,
        compiler_params=_cparams(("parallel", "parallel")),
        name="swa_attention",
    )(sinks, proj, proj, proj, proj, proj, projm, projm,
      tabs_real[0], tabs_real[1], tabs_real[0], tabs_real[1], tabs_meta[0], tabs_meta[1])


def _softplus(z):
    return jnp.maximum(z, 0.0) + jnp.log1p(jnp.exp(-jnp.abs(z)))


def _lru_kernel(xr_ref, prev_ref, hist0_ref, xg_ref, cw_ref, cb_ref, wax_ref, ba_ref, bx_ref, lam_ref, h0_ref,
                y_ref, hl_ref, a_s, u_s, hcar, *, tc):
    c = pl.program_id(2)
    width = a_s.shape[1]
    blocks = width // LRU_BW

    @pl.when(c == 0)
    def _():
        hcar[...] = jnp.broadcast_to(h0_ref[...], hcar.shape)

    first = c == 0
    sub = min(tc, LRU_SUB)

    def gate_rows(sc, carry):
        r0 = pl.multiple_of(sc * sub, sub)
        rs = pl.ds(r0, sub)
        before = pl.ds(pl.multiple_of(jnp.maximum(r0 - N_META, 0), N_META), N_META)
        for nb in range(blocks):
            cs = slice(nb * LRU_BW, (nb + 1) * LRU_BW)
            hist = jnp.where(sc > 0, xr_ref[before, cs], jnp.where(first, hist0_ref[:, cs], prev_ref[:, cs]))
            x = xr_ref[rs, cs].astype(F32)
            ext = jnp.concatenate([hist.astype(F32), x], axis=0)
            cw = cw_ref[:, cs]
            y = cb_ref[:, cs] + x * cw[CONV_W - 1:CONV_W]
            for d in range(1, CONV_W):
                y = y + pltpu.roll(ext, d, 0)[N_META:] * cw[CONV_W - 1 - d:CONV_W - d]
            gates = jnp.dot(y.astype(BF16), wax_ref[nb], preferred_element_type=F32)
            r = jax.nn.sigmoid(gates[:, :LRU_BW] + ba_ref[:, cs])
            ig = jax.nn.sigmoid(gates[:, LRU_BW:] + bx_ref[:, cs])
            log_a = (-RGLRU_C) * r * _softplus(-lam_ref[:, cs])
            a = jnp.exp(log_a)
            a_s[rs, cs] = a
            u_s[rs, cs] = jnp.sqrt(-jnp.tanh(log_a) * (a * a + 1.0)) * (ig * y)
        return carry

    lax.fori_loop(0, tc // sub, gate_rows, 0)
    row = lax.broadcasted_iota(I32, (8, width), 0)

    def body(i, h):
        sl = pl.ds(pl.multiple_of(i * 8, 8), 8)
        a = a_s[sl, :]
        u = u_s[sl, :]
        for d in (1, 2, 4):
            ok = row >= d
            u = jnp.where(ok, a * pltpu.roll(u, d, 0) + u, u)
            a = jnp.where(ok, a * pltpu.roll(a, d, 0), a)
        hs = a * h + u
        u_s[sl, :] = hs
        return jnp.broadcast_to(hs[7:8, :], (8, width))

    h_fin = lax.fori_loop(0, tc // 8, body, hcar[...])
    hcar[...] = h_fin
    hl_ref[0] = h_fin

    def gate_out(sc, carry):
        rs = pl.ds(pl.multiple_of(sc * sub, sub), sub)
        for nb in range(blocks):
            cs = slice(nb * LRU_BW, (nb + 1) * LRU_BW)
            y_ref[rs, cs] = (u_s[rs, cs] * jax.nn.gelu(xg_ref[rs, cs].astype(F32))).astype(BF16)
        return carry

    lax.fori_loop(0, tc // sub, gate_out, 0)


def _lru(proj, hist0, h0, cw, cb, wax, ba, bx, lam, batch, seq, tc):
    nch = seq // tc
    per16 = tc // N_META

    xr_blk = XR_COL // LRU_HALF
    xg_blk = XG_COL // LRU_HALF

    def prev_idx(b, h, c):
        return (jnp.maximum(b * (seq // N_META) + c * per16 - 1, 0), xr_blk + h)

    vec = pl.BlockSpec((1, LRU_HALF), lambda b, h, c: (0, h))
    in_specs = [
        pl.BlockSpec((tc, LRU_HALF), lambda b, h, c: (b * nch + c, xr_blk + h)),
        pl.BlockSpec((N_META, LRU_HALF), prev_idx),
        pl.BlockSpec((N_META, LRU_HALF), lambda b, h, c: (0, h)),
        pl.BlockSpec((tc, LRU_HALF), lambda b, h, c: (b * nch + c, xg_blk + h)),
        pl.BlockSpec((CONV_W, LRU_HALF), lambda b, h, c: (0, h)),
        vec,
        pl.BlockSpec((LRU_BLOCKS // 2, LRU_BW, 2 * LRU_BW), lambda b, h, c: (h, 0, 0)),
        vec, vec, vec, vec,
    ]
    return pl.pallas_call(
        functools.partial(_lru_kernel, tc=tc),
        out_shape=(jax.ShapeDtypeStruct((batch * seq, D_LRU), BF16),
                   jax.ShapeDtypeStruct((batch, 8, D_LRU), F32)),
        grid=(batch, 2, nch),
        in_specs=in_specs,
        out_specs=(pl.BlockSpec((tc, LRU_HALF), lambda b, h, c: (b * nch + c, h)),
                   pl.BlockSpec((1, 8, LRU_HALF), lambda b, h, c: (b, 0, h))),
        scratch_shapes=[pltpu.VMEM((tc, LRU_HALF), F32), pltpu.VMEM((tc, LRU_HALF), F32),
                        pltpu.VMEM((8, LRU_HALF), F32)],
        compiler_params=_cparams(("arbitrary", "arbitrary", "arbitrary")),
        name="conv_rglru",
    )(proj, proj, hist0, proj, cw, cb, wax, ba, bx, lam, h0)


def _merge_kernel(o_ref, y_ref, wa_ref, wl_ref, ga_ref, gl_ref, z_ref):
    ya = jnp.dot(o_ref[...], wa_ref[...], preferred_element_type=F32)
    yl = jnp.dot(y_ref[...], wl_ref[...], preferred_element_type=F32)
    z = jax.nn.sigmoid(ga_ref[...].astype(F32)) * ya + jax.nn.sigmoid(gl_ref[...].astype(F32)) * yl
    z_ref[...] = z.astype(BF16)


def _merge(o_attn, y_lru, wa, wl, proj, tm=512, tn=1024):
    n_tok = o_attn.shape[0]
    ga_blk = GA_COL // tn
    gl_blk = GL_COL // tn
    return pl.pallas_call(
        _merge_kernel,
        out_shape=jax.ShapeDtypeStruct((n_tok, D_MODEL), BF16),
        grid=(D_MODEL // tn, n_tok // tm),
        in_specs=[pl.BlockSpec((tm, D_MODEL), lambda j, i: (i, 0)),
                  pl.BlockSpec((tm, D_MODEL), lambda j, i: (i, 0)),
                  pl.BlockSpec((D_MODEL, tn), lambda j, i: (0, j)),
                  pl.BlockSpec((D_MODEL, tn), lambda j, i: (0, j)),
                  pl.BlockSpec((tm, tn), lambda j, i: (i, ga_blk + j)),
                  pl.BlockSpec((tm, tn), lambda j, i: (i, gl_blk + j))],
        out_specs=pl.BlockSpec((tm, tn), lambda j, i: (i, j)),
        compiler_params=_cparams(("parallel", "parallel")),
        name="gated_merge",
    )(o_attn, y_lru, wa, wl, proj, proj)


def _layernorm(v, g, b):
    mu = jnp.mean(v, axis=-1, keepdims=True)
    dv = v - mu
    var = jnp.mean(dv * dv, axis=-1, keepdims=True)
    return dv * lax.rsqrt(var + LN_EPS) * g + b


def _ln1_kernel(x_ref, z_ref, w_ref, g_ref, b_ref, h_ref, hb_ref):
    mix = jnp.dot(z_ref[...], w_ref[...], preferred_element_type=F32)
    h = _layernorm(DN_ALPHA * x_ref[...] + mix, g_ref[...], b_ref[...])
    h_ref[...] = h
    hb_ref[...] = h.astype(BF16)


def _ln1(x2, z, w_out, g, b, tm=256):
    n_tok = x2.shape[0]
    row = lambda i: (i, 0)
    fixed = lambda i: (0, 0)
    return pl.pallas_call(
        _ln1_kernel,
        out_shape=(jax.ShapeDtypeStruct((n_tok, D_MODEL), F32),
                   jax.ShapeDtypeStruct((n_tok, D_MODEL), BF16)),
        grid=(n_tok // tm,),
        in_specs=[pl.BlockSpec((tm, D_MODEL), row), pl.BlockSpec((tm, D_MODEL), row),
                  pl.BlockSpec((D_MODEL, D_MODEL), fixed),
                  pl.BlockSpec((1, D_MODEL), fixed), pl.BlockSpec((1, D_MODEL), fixed)],
        out_specs=(pl.BlockSpec((tm, D_MODEL), row), pl.BlockSpec((tm, D_MODEL), row)),
        compiler_params=_cparams(("parallel",)),
        name="outproj_ln1",
    )(x2, z, w_out, g, b)


def _router_kernel(h_ref, w_ref, bias_ref, ti_ref, tw_ref, *, tm):
    logits = lax.dot_general(w_ref[...], h_ref[...], (((1,), (1,)), ((), ())), preferred_element_type=F32)
    scores = jax.nn.sigmoid(logits)
    sel = scores + bias_ref[...]
    neg = -jnp.inf
    e_iota = lax.broadcasted_iota(I32, (N_EXPERTS, tm), 0).astype(F32)
    g_iota32 = lax.broadcasted_iota(I32, (GROUP_SIZE, tm), 0).astype(F32)
    gs_rows = []
    for g in range(N_GROUPS):
        blk = sel[g * GROUP_SIZE:(g + 1) * GROUP_SIZE]
        m1 = jnp.max(blk, axis=0, keepdims=True)
        i1 = jnp.min(jnp.where(blk == m1, g_iota32, float(GROUP_SIZE)), axis=0, keepdims=True)
        m2 = jnp.max(jnp.where(g_iota32 == i1, neg, blk), axis=0, keepdims=True)
        gs_rows.append(m1 + m2)
    gs = jnp.concatenate(gs_rows, axis=0)
    g_iota = lax.broadcasted_iota(I32, (N_GROUPS, tm), 0).astype(F32)
    gsel = jnp.zeros((N_GROUPS, tm), F32)
    for _ in range(TOPK_GROUPS):
        m = jnp.max(gs, axis=0, keepdims=True)
        idx = jnp.min(jnp.where(gs == m, g_iota, float(N_GROUPS)), axis=0, keepdims=True)
        hit = g_iota == idx
        gsel = jnp.where(hit, 1.0, gsel)
        gs = jnp.where(hit, neg, gs)
    masked = jnp.concatenate(
        [jnp.where(gsel[g:g + 1] > 0.0, sel[g * GROUP_SIZE:(g + 1) * GROUP_SIZE], neg) for g in range(N_GROUPS)],
        axis=0)
    tis, tws = [], []
    for _ in range(TOP_K):
        m = jnp.max(masked, axis=0, keepdims=True)
        idx = jnp.min(jnp.where(masked == m, e_iota, float(N_EXPERTS)), axis=0, keepdims=True)
        hit = e_iota == idx
        tis.append(idx)
        tws.append(jnp.sum(jnp.where(hit, scores, 0.0), axis=0, keepdims=True))
        masked = jnp.where(hit, neg, masked)
    tw = jnp.concatenate(tws, axis=0)
    ti_ref[...] = jnp.concatenate(tis, axis=0).astype(I32)
    tw_ref[...] = tw / jnp.sum(tw, axis=0, keepdims=True) * ROUTED_SCALE


def _router(hb, wr_t, bias, tm=512):
    n_tok = hb.shape[0]
    return pl.pallas_call(
        functools.partial(_router_kernel, tm=tm),
        out_shape=(jax.ShapeDtypeStruct((TOP_K, n_tok), I32), jax.ShapeDtypeStruct((TOP_K, n_tok), F32)),
        grid=(n_tok // tm,),
        in_specs=[pl.BlockSpec((tm, D_MODEL), lambda i: (i, 0)),
                  pl.BlockSpec((N_EXPERTS, D_MODEL), lambda i: (0, 0)),
                  pl.BlockSpec((N_EXPERTS, 1), lambda i: (0, 0))],
        out_specs=(pl.BlockSpec((TOP_K, tm), lambda i: (0, i)), pl.BlockSpec((TOP_K, tm), lambda i: (0, i))),
        compiler_params=_cparams(("parallel",)),
        name="router_topk",
    )(hb, wr_t, bias)


def _expert_kernel(nblk_ref, bst_ref, xs_ref, wg_ref, wu_ref, wd_ref, ys_ref,
                   xbuf, obuf, isem, osem, wgb, wub, wdb):
    e = pl.program_id(0)
    nb = nblk_ref[e]
    b0 = bst_ref[e]
    nused = bst_ref[N_EXPERTS - 1] + nblk_ref[N_EXPERTS - 1]
    ahead = EXPERT_RING - 1

    def rows(g):
        return pl.ds(pl.multiple_of(g * EXPERT_BLOCK, EXPERT_BLOCK), EXPERT_BLOCK)

    def in_copy(g):
        slot = g & (EXPERT_RING - 1)
        return pltpu.make_async_copy(xs_ref.at[rows(g), :], xbuf.at[slot], isem.at[slot])

    def out_copy(g):
        slot = g & (EXPERT_RING - 1)
        return pltpu.make_async_copy(obuf.at[slot], ys_ref.at[rows(g), :], osem.at[slot])

    @pl.when(nb > 0)
    def _():
        @pl.when(b0 == 0)
        def _():
            for q in range(ahead):
                @pl.when(q < nused)
                def _(q=q):
                    in_copy(q).start(priority=1)

        wgb[...] = wg_ref[0].astype(BF16)
        wub[...] = wu_ref[0].astype(BF16)
        wdb[...] = wd_ref[0].astype(BF16)

        def body(j, carry):
            g = b0 + j
            slot = g & (EXPERT_RING - 1)

            @pl.when(g + ahead < nused)
            def _():
                in_copy(g + ahead).start(priority=1)

            in_copy(g).wait()
            x = xbuf[slot].astype(BF16)
            hid = jax.nn.silu(jnp.dot(x, wgb[...], preferred_element_type=F32)) * jnp.dot(
                x, wub[...], preferred_element_type=F32)
            out = jnp.dot(hid.astype(BF16), wdb[...], preferred_element_type=F32)

            @pl.when(g >= EXPERT_RING)
            def _():
                out_copy(g - EXPERT_RING).wait()

            obuf[slot] = out
            out_copy(g).start(priority=1)
            return carry

        lax.fori_loop(0, nb, body, 0)

    @pl.when(e == N_EXPERTS - 1)
    def _():
        for q in range(EXPERT_RING, 0, -1):
            @pl.when(nused >= q)
            def _(q=q):
                out_copy(nused - q).wait()


def _experts(nblk, bst, xs, wg, wu, wd):
    grid_spec = pltpu.PrefetchScalarGridSpec(
        num_scalar_prefetch=2, grid=(N_EXPERTS,),
        in_specs=[pl.BlockSpec(memory_space=pl.ANY),
                  pl.BlockSpec((1, D_MODEL, D_EXPERT), lambda e, nb, bs: (e, 0, 0)),
                  pl.BlockSpec((1, D_MODEL, D_EXPERT), lambda e, nb, bs: (e, 0, 0)),
                  pl.BlockSpec((1, D_EXPERT, D_MODEL), lambda e, nb, bs: (e, 0, 0))],
        out_specs=pl.BlockSpec(memory_space=pl.ANY),
        scratch_shapes=[pltpu.VMEM((EXPERT_RING, EXPERT_BLOCK, D_MODEL), F32),
                        pltpu.VMEM((EXPERT_RING, EXPERT_BLOCK, D_MODEL), F32),
                        pltpu.SemaphoreType.DMA((EXPERT_RING,)), pltpu.SemaphoreType.DMA((EXPERT_RING,)),
                        pltpu.VMEM((D_MODEL, D_EXPERT), BF16), pltpu.VMEM((D_MODEL, D_EXPERT), BF16),
                        pltpu.VMEM((D_EXPERT, D_MODEL), BF16)])
    return pl.pallas_call(
        _expert_kernel,
        out_shape=jax.ShapeDtypeStruct(xs.shape, F32),
        grid_spec=grid_spec,
        input_output_aliases={2: 0},
        compiler_params=_cparams(("arbitrary",)),
        name="routed_experts",
    )(nblk, bst, xs, wg, wu, wd)


COMBINE_CHUNK = 16


def _combine_kernel(dc_ref, dn_ref, ys_ref, tw_ref, h_ref, sh_ref, g_ref, b_ref, o_ref, gbuf, sem, *, tm):
    i = pl.program_id(0)
    nsteps = pl.num_programs(0)
    nchunks = tm // COMBINE_CHUNK

    def issue_tile(d_ref, s):
        def body(t, carry):
            for k in range(TOP_K):
                pltpu.make_async_copy(ys_ref.at[pl.ds(d_ref[0, k, t], 1), :],
                                      gbuf.at[s, k, pl.ds(t, 1), :], sem.at[s]).start(priority=k % 2)
            return carry
        lax.fori_loop(0, tm, body, 0, unroll=2)

    def wait_slabs(s):
        for k in range(TOP_K):
            pltpu.make_async_copy(ys_ref.at[pl.ds(0, tm), :], gbuf.at[s, k], sem.at[s]).wait()

    @pl.when(i == 0)
    def _():
        issue_tile(dc_ref, 0)

    g = g_ref[...]
    b = b_ref[...]

    def reduce_tile(s):
        def chunk(c, carry):
            rs = pl.ds(pl.multiple_of(c * COMBINE_CHUNK, COMBINE_CHUNK), COMBINE_CHUNK)
            tw = tw_ref[rs, :]
            routed = tw[:, 0:1] * gbuf[s, 0, rs, :]
            for k in range(1, TOP_K):
                routed = routed + tw[:, k:k + 1] * gbuf[s, k, rs, :]
            v = DN_ALPHA * h_ref[rs, :] + (routed + sh_ref[rs, :].astype(F32))
            o_ref[rs, :] = _layernorm(v, g, b)
            return carry
        lax.fori_loop(0, nchunks, chunk, 0)

    for s in (0, 1):
        @pl.when((i & 1) == s)
        def _(s=s):
            @pl.when(i + 1 < nsteps)
            def _():
                issue_tile(dn_ref, 1 - s)

            wait_slabs(s)
            reduce_tile(s)


def _combine(dest3, ys, topw, h1, shared, g, b):
    n_tok = h1.shape[0]
    nsteps, _, tm = dest3.shape
    row = lambda i: (i, 0)
    fixed = lambda i: (0, 0)
    return pl.pallas_call(
        functools.partial(_combine_kernel, tm=tm),
        out_shape=jax.ShapeDtypeStruct((n_tok, D_MODEL), F32),
        grid=(nsteps,),
        in_specs=[pl.BlockSpec((1, TOP_K, tm), lambda i: (i, 0, 0), memory_space=pltpu.SMEM),
                  pl.BlockSpec((1, TOP_K, tm), lambda i: (jnp.minimum(i + 1, nsteps - 1), 0, 0),
                               memory_space=pltpu.SMEM),
                  pl.BlockSpec(memory_space=pl.ANY),
                  pl.BlockSpec((tm, TOP_K), row),
                  pl.BlockSpec((tm, D_MODEL), row), pl.BlockSpec((tm, D_MODEL), row),
                  pl.BlockSpec((1, D_MODEL), fixed), pl.BlockSpec((1, D_MODEL), fixed)],
        out_specs=pl.BlockSpec((tm, D_MODEL), row),
        scratch_shapes=[pltpu.VMEM((2, TOP_K, tm, D_MODEL), F32), pltpu.SemaphoreType.DMA((2,))],
        compiler_params=_cparams(("arbitrary",)),
        name="combine_ln2",
    )(dest3, dest3, ys, topw, h1, shared, g, b)


PLAN_TM = 128
LANES = 128


def _small_int_halves(v):
    hi = jnp.floor(v * (1.0 / 256.0))
    return hi.astype(BF16), (v - hi * 256.0).astype(BF16)


def _plan_kernel(ti_ref, dest_ref, info_ref, rank_s, cntc_s, cntr_s, pst_s, *, tm):
    p = pl.program_id(0)
    i = pl.program_id(1)
    ti = ti_ref[...]
    e_iota = lax.broadcasted_iota(I32, (N_EXPERTS, tm), 0)
    reps = tm // LANES

    @pl.when(jnp.logical_and(p == 0, i == 0))
    def _():
        cntc_s[...] = jnp.zeros(cntc_s.shape, F32)
        cntr_s[...] = jnp.zeros(cntr_s.shape, F32)

    @pl.when(p == 0)
    def _():
        hits = [e_iota == ti[k:k + 1] for k in range(TOP_K)]
        m = jnp.where(hits[0], 1.0, 0.0)
        for k in range(1, TOP_K):
            m = m + jnp.where(hits[k], 1.0, 0.0)
        mb = m.astype(BF16)
        earlier = lax.broadcasted_iota(I32, (tm, tm), 0) < lax.broadcasted_iota(I32, (tm, tm), 1)
        pfx = jnp.dot(mb, jnp.where(earlier, 1.0, 0.0).astype(BF16), preferred_element_type=F32)
        val = pfx + jnp.concatenate([cntc_s[...]] * reps, axis=1)
        rank_s[i] = jnp.concatenate(
            [jnp.sum(jnp.where(hits[k], val, 0.0), axis=0, keepdims=True) for k in range(TOP_K)], axis=0)
        cntc_s[...] += jnp.dot(mb, jnp.ones((tm, LANES), BF16), preferred_element_type=F32)
        cntr_s[...] += lax.dot_general(jnp.ones((8, tm), BF16), mb, (((1,), (1,)), ((), ())),
                                       preferred_element_type=F32)

    @pl.when(jnp.logical_and(p == 1, i == 0))
    def _():
        def n_blocks_of(cnt):
            return jnp.right_shift(cnt.astype(I32) + (EXPERT_BLOCK - 1), EXPERT_BLOCK.bit_length() - 1).astype(F32)

        ee0 = lax.broadcasted_iota(I32, (N_EXPERTS, N_EXPERTS), 0)
        ee1 = lax.broadcasted_iota(I32, (N_EXPERTS, N_EXPERTS), 1)
        hi, lo = _small_int_halves(n_blocks_of(cntc_s[...]))
        lower = jnp.where(ee1 < ee0, 1.0, 0.0).astype(BF16)
        bst_c = 256.0 * jnp.dot(lower, hi, preferred_element_type=F32) + jnp.dot(lower, lo, preferred_element_type=F32)
        pst_s[...] = bst_c * float(EXPERT_BLOCK)
        cnt_r = cntr_s[...]
        nb_r = n_blocks_of(cnt_r)
        hi, lo = _small_int_halves(nb_r)
        upper = jnp.where(ee0 < ee1, 1.0, 0.0).astype(BF16)
        bst_r = 256.0 * jnp.dot(hi, upper, preferred_element_type=F32) + jnp.dot(lo, upper, preferred_element_type=F32)
        sel = lax.broadcasted_iota(I32, (8, N_EXPERTS), 0)
        info = jnp.where(sel == 0, cnt_r, jnp.where(sel == 1, nb_r, jnp.where(sel == 2, bst_r, 0.0)))
        info_ref[...] = info.astype(I32)

    @pl.when(p == 1)
    def _():
        pst = jnp.concatenate([pst_s[...]] * reps, axis=1)
        off = jnp.concatenate(
            [jnp.sum(jnp.where(e_iota == ti[k:k + 1], pst, 0.0), axis=0, keepdims=True) for k in range(TOP_K)], axis=0)
        dest_ref[0] = (rank_s[i] + off).astype(I32)


def _plan(topi_t, tm=PLAN_TM):
    n_tok = topi_t.shape[1]
    nt = n_tok // tm
    return pl.pallas_call(
        functools.partial(_plan_kernel, tm=tm),
        out_shape=(jax.ShapeDtypeStruct((nt, TOP_K, tm), I32), jax.ShapeDtypeStruct((8, N_EXPERTS), I32)),
        grid=(2, nt),
        in_specs=[pl.BlockSpec((TOP_K, tm), lambda p, i: (0, i))],
        out_specs=(pl.BlockSpec((1, TOP_K, tm), lambda p, i: (i * p, 0, 0)),
                   pl.BlockSpec((8, N_EXPERTS), lambda p, i: (0, 0))),
        scratch_shapes=[pltpu.VMEM((nt, TOP_K, tm), F32), pltpu.VMEM((N_EXPERTS, LANES), F32),
                        pltpu.VMEM((8, N_EXPERTS), F32), pltpu.VMEM((N_EXPERTS, LANES), F32)],
        compiler_params=_cparams(("arbitrary", "arbitrary")),
        name="expert_row_plan",
    )(topi_t)


def _dispatch_kernel(cnt_ref, nblk_ref, bst_ref, dest_ref, hp_ref, hb_ref, wg_ref, wu_ref, wd_ref, xs_ref, sh_ref,
                     zrow, sem, zsem, tsem, *, tm, epg, tpg, n_blocks):
    i = pl.program_id(0)

    @pl.when(i == 0)
    def _():
        zrow[...] = jnp.zeros(zrow.shape, F32)

    nused = bst_ref[N_EXPERTS - 1] + nblk_ref[N_EXPERTS - 1]
    t_lo = jnp.minimum(nused + i * tpg, n_blocks)
    t_hi = jnp.minimum(nused + (i + 1) * tpg, n_blocks)

    def tail_copy(blk):
        return pltpu.make_async_copy(
            zrow, xs_ref.at[pl.ds(pl.multiple_of(blk * EXPERT_BLOCK, EXPERT_BLOCK), EXPERT_BLOCK), :], tsem)

    def tfill(blk, carry):
        tail_copy(blk).start()
        return carry

    def twait(blk, carry):
        tail_copy(blk).wait()
        return carry

    lax.fori_loop(t_lo, t_hi, tfill, 0)

    def scatter(t, carry):
        for k in range(TOP_K):
            pltpu.make_async_copy(hp_ref.at[pl.ds(t, 1), :], xs_ref.at[pl.ds(dest_ref[0, k, t], 1), :],
                                  sem).start(priority=k % 2)
        return carry

    lax.fori_loop(0, tm, scatter, 0, unroll=2)

    def zfill(r, carry):
        pltpu.make_async_copy(zrow.at[pl.ds(0, 1), :], xs_ref.at[pl.ds(r, 1), :], zsem).start()
        return carry

    def zwait(r, carry):
        pltpu.make_async_copy(zrow.at[pl.ds(0, 1), :], xs_ref.at[pl.ds(0, 1), :], zsem).wait()
        return carry

    bounds = []
    for q in range(epg):
        e = i * epg + q
        lo = bst_ref[e] * EXPERT_BLOCK + cnt_ref[e]
        hi = (bst_ref[e] + nblk_ref[e]) * EXPERT_BLOCK
        bounds.append((lo, hi))
        lax.fori_loop(lo, hi, zfill, 0)

    hb = hb_ref[...]
    hid = jax.nn.silu(jnp.dot(hb, wg_ref[...], preferred_element_type=F32)) * jnp.dot(
        hb, wu_ref[...], preferred_element_type=F32)
    sh_ref[...] = jnp.dot(hid.astype(BF16), wd_ref[...], preferred_element_type=F32).astype(BF16)

    for k in range(TOP_K):
        pltpu.make_async_copy(hp_ref, xs_ref.at[pl.ds(0, tm), :], sem).wait()
    for lo, hi in bounds:
        lax.fori_loop(lo, hi, zwait, 0)
    lax.fori_loop(t_lo, t_hi, twait, 0)


def _dispatch(cnt, nblk, bst, dest3, hp, hb, wgs, wus, wds, n_rows):
    nsteps, _, tm = dest3.shape
    n_tok = hp.shape[0]
    n_blocks = n_rows // EXPERT_BLOCK
    assert N_EXPERTS % nsteps == 0
    row = lambda i, *_: (i, 0)
    fixed = lambda i, *_: (0, 0)
    grid_spec = pltpu.PrefetchScalarGridSpec(
        num_scalar_prefetch=3, grid=(nsteps,),
        in_specs=[pl.BlockSpec((1, TOP_K, tm), lambda i, *_: (i, 0, 0), memory_space=pltpu.SMEM),
                  pl.BlockSpec((tm, D_MODEL), row), pl.BlockSpec((tm, D_MODEL), row),
                  pl.BlockSpec((D_MODEL, D_EXPERT), fixed), pl.BlockSpec((D_MODEL, D_EXPERT), fixed),
                  pl.BlockSpec((D_EXPERT, D_MODEL), fixed)],
        out_specs=(pl.BlockSpec(memory_space=pl.ANY), pl.BlockSpec((tm, D_MODEL), row)),
        scratch_shapes=[pltpu.VMEM((EXPERT_BLOCK, D_MODEL), F32), pltpu.SemaphoreType.DMA(()),
                        pltpu.SemaphoreType.DMA(()), pltpu.SemaphoreType.DMA(())])
    return pl.pallas_call(
        functools.partial(_dispatch_kernel, tm=tm, epg=N_EXPERTS // nsteps, tpg=-(-n_blocks // nsteps),
                          n_blocks=n_blocks),
        out_shape=(jax.ShapeDtypeStruct((n_rows, D_MODEL), F32), jax.ShapeDtypeStruct((n_tok, D_MODEL), BF16)),
        grid_spec=grid_spec,
        compiler_params=_cparams(("arbitrary",)),
        name="dispatch_shared",
    )(cnt, nblk, bst, dest3, hp, hb, wgs, wus, wds)


def _rope_tables(n_pos):
    half = ROT_DIM // 2
    inv_freq = ROPE_THETA ** (-jnp.arange(half, dtype=F32) / half)
    ang = jnp.arange(n_pos).astype(F32)[:, None] * inv_freq[None, :]
    cos, sin = jnp.cos(ang), jnp.sin(ang)
    c = jnp.concatenate([cos, cos, jnp.ones((n_pos, HEAD_DIM - ROT_DIM), F32)], axis=1)
    s = jnp.concatenate([sin, sin, jnp.zeros((n_pos, HEAD_DIM - ROT_DIM), F32)], axis=1)
    return c, s


def kernel(x, meta_tokens, w_in, conv_w, conv_b, w_rg_a, b_rg_a, w_rg_x, b_rg_x, rg_lambda, attn_sinks, w_o_attn, w_o_lru, w_out, ln1_g, ln1_b, w_router, router_bias, w_gate_e, w_up_e, w_down_e, w_gate_s, w_up_s, w_down_s, ln2_g, ln2_b):
    batch, seq, _ = x.shape
    n_tok = batch * seq
    assert w_in.shape[0] == 1 and seq % 512 == 0 and meta_tokens.shape[0] == N_META
    x2 = x.reshape(n_tok, D_MODEL)
    row = lambda v: v.reshape(1, -1).astype(F32)

    wax = jnp.concatenate([w_rg_a[0], w_rg_x[0]], axis=-1).astype(BF16)
    c_t, s_t = _rope_tables(N_META + seq)
    tabs_meta = (c_t[:N_META], s_t[:N_META])
    tabs_real = (c_t[N_META:], s_t[N_META:])

    projm = _matmul(meta_tokens, w_in[0], BF16, N_META, 1024)
    lru_args = (conv_w[0], row(conv_b[0]), wax, row(b_rg_a[0]), row(b_rg_x[0]), row(rg_lambda[0]))
    _, h_meta = _lru(projm, jnp.zeros((N_META, D_LRU), BF16), jnp.zeros((1, D_LRU), F32), *lru_args,
                     batch=1, seq=N_META, tc=N_META)

    proj = _matmul(x2, w_in[0], BF16, 1024, 1024)
    o_attn = _attention(proj, projm, attn_sinks[0].astype(F32), tabs_real, tabs_meta, batch, seq)
    y_lru, _ = _lru(proj, projm[:, XR_COL:XR_COL + D_LRU], h_meta[0, 7:8], *lru_args, batch=batch, seq=seq, tc=1024)
    z = _merge(o_attn, y_lru, w_o_attn[0].astype(BF16), w_o_lru[0].astype(BF16), proj)
    h1, h1b = _ln1(x2, z, w_out[0].astype(BF16), row(ln1_g[0]), row(ln1_b[0]))

    topi_t, topw_t = _router(h1b, w_router[0].T.astype(BF16), router_bias[0].reshape(N_EXPERTS, 1).astype(F32))
    dest3, info = _plan(topi_t)
    cnt, nblk, bst = info[0], info[1], info[2]
    n_rows = -(-(n_tok * TOP_K + N_EXPERTS * (EXPERT_BLOCK - 1)) // EXPERT_BLOCK) * EXPERT_BLOCK
    xs, shared = _dispatch(cnt, nblk, bst, dest3, h1, h1b,w_gate_s[0].astype(BF16), w_up_s[0].astype(BF16),
                           w_down_s[0].astype(BF16), n_rows)
    ys = _experts(nblk, bst, xs, w_gate_e[0], w_up_e[0], w_down_e[0])
    out = _combine(dest3, ys, topw_t.T, h1, shared, row(ln2_g[0]), row(ln2_b[0]))
    return out.reshape(batch, seq, D_MODEL)
```

```python
import functools

import jax
import jax.numpy as jnp
from jax import lax
from jax.experimental import pallas as pl
from jax.experimental.pallas import tpu as pltpu

F32 = jnp.float32
BF16 = jnp.bfloat16
U32 = jnp.uint32
I32 = jnp.int32

D_MODEL = 2048
N_META = 16
N_Q_HEADS = 16
N_KV_HEADS = 4
HEAD_DIM = 128
Q_GROUP = N_Q_HEADS // N_KV_HEADS
ROT_DIM = HEAD_DIM // 4
ROPE_THETA = 500000.0
ATTN_BLOCK = 128
D_LRU = D_MODEL
LRU_BLOCKS = 16
LRU_BW = D_LRU // LRU_BLOCKS
CONV_W = 4
RGLRU_C = 8.0
N_EXPERTS = 256
TOP_K = 8
N_GROUPS = 8
GROUP_SIZE = N_EXPERTS // N_GROUPS
TOPK_GROUPS = 4
D_EXPERT = 512
ROUTED_SCALE = 2.5
EXPERT_BLOCK = 128
EXPERT_RING = 8
DN_ALPHA = 2.0 ** 0.25
LN_EPS = 1e-5
NEG_INF = -1e30
KV_DIM = N_KV_HEADS * HEAD_DIM
Q_DIM = N_Q_HEADS * HEAD_DIM
K_COLBLK = Q_DIM // KV_DIM
V_COLBLK = K_COLBLK + 1
XR_COL = Q_DIM + 2 * KV_DIM
XG_COL = XR_COL + D_LRU
GA_COL = XG_COL + D_LRU
GL_COL = GA_COL + D_MODEL
LRU_HALF = D_LRU // 2
LRU_SUB = 256
VMEM_LIMIT = 50 * 1024 * 1024
EXPERT_VMEM_LIMIT = (2 * 3 * D_MODEL * D_EXPERT * 4 + 3 * D_MODEL * D_EXPERT * 2
                     + 2 * EXPERT_RING * EXPERT_BLOCK * D_MODEL * 4 + 8 * 1024 * 1024)


def _cparams(sem, vmem_limit=VMEM_LIMIT):
    return pltpu.CompilerParams(dimension_semantics=sem, vmem_limit_bytes=vmem_limit)


def _mm_kernel(a_ref, b_ref, o_ref, b_bf):
    @pl.when(pl.program_id(1) == 0)
    def _():
        b_bf[...] = b_ref[...].astype(BF16)

    o_ref[...] = jnp.dot(a_ref[...].astype(BF16), b_bf[...], preferred_element_type=F32).astype(o_ref.dtype)


def _matmul(a, b, out_dtype, tm, tn):
    m, k = a.shape
    n = b.shape[1]
    return pl.pallas_call(
        _mm_kernel,
        out_shape=jax.ShapeDtypeStruct((m, n), out_dtype),
        grid=(n // tn, m // tm),
        in_specs=[pl.BlockSpec((tm, k), lambda j, i: (i, 0)),
                  pl.BlockSpec((k, tn), lambda j, i: (0, j))],
        out_specs=pl.BlockSpec((tm, tn), lambda j, i: (i, j)),
        scratch_shapes=[pltpu.VMEM((k, tn), BF16)],
        compiler_params=_cparams(("arbitrary", "arbitrary")),
        name="dense_matmul",
    )(a, b)


def _rope(x, rot, c, s):
    return x.astype(F32) * c + jnp.dot(x, rot, preferred_element_type=F32) * s


def _attn_kernel(sink_ref, q_ref, kc_ref, kp_ref, vc_ref, vp_ref, km_ref, vm_ref,
                 cq_ref, sq_ref, cp_ref, sp_ref, cm_ref, sm_ref, o_ref, *, tq):
    n = pl.program_id(1)
    nsub = tq // ATTN_BLOCK
    scale = HEAD_DIM ** -0.5
    cq, sq = cq_ref[...], sq_ref[...]
    half = ROT_DIM // 2
    rj = lax.broadcasted_iota(I32, (HEAD_DIM, HEAD_DIM), 0)
    ri = lax.broadcasted_iota(I32, (HEAD_DIM, HEAD_DIM), 1)
    rot = (jnp.where(jnp.logical_and(ri < half, rj == ri + half), -1.0, 0.0)
           + jnp.where(jnp.logical_and(jnp.logical_and(ri >= half, ri < ROT_DIM), rj == ri - half), 1.0, 0.0)
           ).astype(BF16)
    rows = Q_GROUP * ATTN_BLOCK
    r_idx = lax.broadcasted_iota(I32, (rows, ATTN_BLOCK), 0) % ATTN_BLOCK
    c_idx = lax.broadcasted_iota(I32, (rows, ATTN_BLOCK), 1)
    cur_ok = c_idx <= r_idx
    prev_ok = c_idx > r_idx
    first_prev_ok = (c_idx - r_idx) > jnp.where(n > 0, 0, ATTN_BLOCK)
    meta_ok = c_idx < N_META
    pad_rows = jnp.zeros((ATTN_BLOCK - N_META, HEAD_DIM), BF16)
    ones_blk = jnp.ones((3 * ATTN_BLOCK, HEAD_DIM), BF16)
    dn = (((1,), (1,)), ((), ()))
    for g in range(N_KV_HEADS):
        hs = slice(g * HEAD_DIM, (g + 1) * HEAD_DIM)
        k_g = _rope(kc_ref[:, hs], rot, cq, sq).astype(BF16)
        kp_g = _rope(kp_ref[:, hs], rot, cp_ref[...], sp_ref[...]).astype(BF16)
        km_g = jnp.concatenate([_rope(km_ref[:, hs], rot, cm_ref[...], sm_ref[...]).astype(BF16), pad_rows], axis=0)
        v_g = vc_ref[:, hs]
        vp_g = vp_ref[:, hs]
        vm_g = jnp.concatenate([vm_ref[:, hs], pad_rows], axis=0)
        sink = jnp.concatenate(
            [jnp.full((ATTN_BLOCK, 1), sink_ref[g * Q_GROUP + j], F32) for j in range(Q_GROUP)], axis=0)
        for sub in range(nsub):
            rs = slice(sub * ATTN_BLOCK, (sub + 1) * ATTN_BLOCK)
            q_st = jnp.concatenate(
                [_rope(q_ref[rs, (g * Q_GROUP + j) * HEAD_DIM:(g * Q_GROUP + j + 1) * HEAD_DIM], rot,
                       cq[rs], sq[rs]) for j in range(Q_GROUP)], axis=0).astype(BF16)
            if sub == 0:
                k_all = jnp.concatenate([kp_g, k_g[rs], km_g], axis=0)
                v_all = jnp.concatenate([vp_g, v_g[rs], vm_g], axis=0)
                p_ok = first_prev_ok
            else:
                both = slice((sub - 1) * ATTN_BLOCK, (sub + 1) * ATTN_BLOCK)
                k_all = jnp.concatenate([k_g[both], km_g], axis=0)
                v_all = jnp.concatenate([v_g[both], vm_g], axis=0)
                p_ok = prev_ok
            sc = lax.dot_general(q_st, k_all, dn, preferred_element_type=F32) * scale
            s_p = jnp.where(p_ok, sc[:, :ATTN_BLOCK], NEG_INF)
            s_c = jnp.where(cur_ok, sc[:, ATTN_BLOCK:2 * ATTN_BLOCK], NEG_INF)
            s_m = jnp.where(meta_ok, sc[:, 2 * ATTN_BLOCK:], NEG_INF)
            m = jnp.maximum(jnp.max(jnp.maximum(jnp.maximum(s_p, s_c), s_m), axis=1, keepdims=True), sink)
            p = jnp.concatenate([jnp.exp(s_p - m), jnp.exp(s_c - m), jnp.exp(s_m - m)], axis=1).astype(BF16)
            od = jnp.dot(p, jnp.concatenate([v_all, ones_blk], axis=1), preferred_element_type=F32)
            o = od[:, :HEAD_DIM] / (od[:, HEAD_DIM:] + jnp.exp(sink - m))
            for j in range(Q_GROUP):
                h = g * Q_GROUP + j
                o_ref[rs, h * HEAD_DIM:(h + 1) * HEAD_DIM] = o[j * ATTN_BLOCK:(j + 1) * ATTN_BLOCK].astype(BF16)


def _attention(proj, projm, sinks, tabs_real, tabs_meta, batch, seq, tq=512):
    n_tok = batch * seq
    nq = seq // tq
    sub_per = tq // ATTN_BLOCK

    def prev_blk(b, n):
        return jnp.maximum(b * (seq // ATTN_BLOCK) + n * sub_per - 1, 0)

    tab_spec = pl.BlockSpec((tq, HEAD_DIM), lambda b, n, s: (n, 0))
    tabp_spec = pl.BlockSpec((ATTN_BLOCK, HEAD_DIM), lambda b, n, s: (jnp.maximum(n * sub_per - 1, 0), 0))
    tabm_spec = pl.BlockSpec((N_META, HEAD_DIM), lambda b, n, s: (0, 0))
    in_specs = [
        pl.BlockSpec((tq, D_MODEL), lambda b, n, s: (b * nq + n, 0)),
        pl.BlockSpec((tq, KV_DIM), lambda b, n, s: (b * nq + n, K_COLBLK)),
        pl.BlockSpec((ATTN_BLOCK, KV_DIM), lambda b, n, s: (prev_blk(b, n), K_COLBLK)),
        pl.BlockSpec((tq, KV_DIM), lambda b, n, s: (b * nq + n, V_COLBLK)),
        pl.BlockSpec((ATTN_BLOCK, KV_DIM), lambda b, n, s: (prev_blk(b, n), V_COLBLK)),
        pl.BlockSpec((N_META, KV_DIM), lambda b, n, s: (0, K_COLBLK)),
        pl.BlockSpec((N_META, KV_DIM), lambda b, n, s: (0, V_COLBLK)),
        tab_spec, tab_spec, tabp_spec, tabp_spec, tabm_spec, tabm_spec,
    ]
    grid_spec = pltpu.PrefetchScalarGridSpec(
        num_scalar_prefetch=1, grid=(batch, nq), in_specs=in_specs,
        out_specs=pl.BlockSpec((tq, D_MODEL), lambda b, n, s: (b * nq + n, 0)))
    return pl.pallas_call(
        functools.partial(_attn_kernel, tq=tq),
        out_shape=jax.ShapeDtypeStruct((n_tok, D_MODEL), BF16),
        grid_spec=grid_spec,
        compiler_params=_cparams(("parallel", "parallel")),
        name="swa_attention",
    )(sinks, proj, proj, proj, proj, proj, projm, projm,
      tabs_real[0], tabs_real[1], tabs_real[0], tabs_real[1], tabs_meta[0], tabs_meta[1])


def _softplus(z):
    return jnp.maximum(z, 0.0) + jnp.log1p(jnp.exp(-jnp.abs(z)))


def _lru_kernel(xr_ref, prev_ref, hist0_ref, xg_ref, cw_ref, cb_ref, wax_ref, ba_ref, bx_ref, lam_ref, h0_ref,
                y_ref, hl_ref, a_s, u_s, hcar, *, tc):
    c = pl.program_id(2)
    width = a_s.shape[1]
    blocks = width // LRU_BW

    @pl.when(c == 0)
    def _():
        hcar[...] = jnp.broadcast_to(h0_ref[...], hcar.shape)

    first = c == 0
    sub = min(tc, LRU_SUB)

    def gate_rows(sc, carry):
        r0 = pl.multiple_of(sc * sub, sub)
        rs = pl.ds(r0, sub)
        before = pl.ds(pl.multiple_of(jnp.maximum(r0 - N_META, 0), N_META), N_META)
        for nb in range(blocks):
            cs = slice(nb * LRU_BW, (nb + 1) * LRU_BW)
            hist = jnp.where(sc > 0, xr_ref[before, cs], jnp.where(first, hist0_ref[:, cs], prev_ref[:, cs]))
            x = xr_ref[rs, cs].astype(F32)
            ext = jnp.concatenate([hist.astype(F32), x], axis=0)
            cw = cw_ref[:, cs]
            y = cb_ref[:, cs] + x * cw[CONV_W - 1:CONV_W]
            for d in range(1, CONV_W):
                y = y + pltpu.roll(ext, d, 0)[N_META:] * cw[CONV_W - 1 - d:CONV_W - d]
            gates = jnp.dot(y.astype(BF16), wax_ref[nb], preferred_element_type=F32)
            r = jax.nn.sigmoid(gates[:, :LRU_BW] + ba_ref[:, cs])
            ig = jax.nn.sigmoid(gates[:, LRU_BW:] + bx_ref[:, cs])
            log_a = (-RGLRU_C) * r * _softplus(-lam_ref[:, cs])
            a = jnp.exp(log_a)
            a_s[rs, cs] = a
            u_s[rs, cs] = jnp.sqrt(-jnp.tanh(log_a) * (a * a + 1.0)) * (ig * y)
        return carry

    lax.fori_loop(0, tc // sub, gate_rows, 0)
    row = lax.broadcasted_iota(I32, (8, width), 0)

    def body(i, h):
        sl = pl.ds(pl.multiple_of(i * 8, 8), 8)
        a = a_s[sl, :]
        u = u_s[sl, :]
        for d in (1, 2, 4):
            ok = row >= d
            u = jnp.where(ok, a * pltpu.roll(u, d, 0) + u, u)
            a = jnp.where(ok, a * pltpu.roll(a, d, 0), a)
        hs = a * h + u
        u_s[sl, :] = hs
        return jnp.broadcast_to(hs[7:8, :], (8, width))

    h_fin = lax.fori_loop(0, tc // 8, body, hcar[...])
    hcar[...] = h_fin
    hl_ref[0] = h_fin

    def gate_out(sc, carry):
        rs = pl.ds(pl.multiple_of(sc * sub, sub), sub)
        for nb in range(blocks):
            cs = slice(nb * LRU_BW, (nb + 1) * LRU_BW)
            y_ref[rs, cs] = (u_s[rs, cs] * jax.nn.gelu(xg_ref[rs, cs].astype(F32))).astype(BF16)
        return carry

    lax.fori_loop(0, tc // sub, gate_out, 0)


def _lru(proj, hist0, h0, cw, cb, wax, ba, bx, lam, batch, seq, tc):
    nch = seq // tc
    per16 = tc // N_META

    xr_blk = XR_COL // LRU_HALF
    xg_blk = XG_COL // LRU_HALF

    def prev_idx(b, h, c):
        return (jnp.maximum(b * (seq // N_META) + c * per16 - 1, 0), xr_blk + h)

    vec = pl.BlockSpec((1, LRU_HALF), lambda b, h, c: (0, h))
    in_specs = [
        pl.BlockSpec((tc, LRU_HALF), lambda b, h, c: (b * nch + c, xr_blk + h)),
        pl.BlockSpec((N_META, LRU_HALF), prev_idx),
        pl.BlockSpec((N_META, LRU_HALF), lambda b, h, c: (0, h)),
        pl.BlockSpec((tc, LRU_HALF), lambda b, h, c: (b * nch + c, xg_blk + h)),
        pl.BlockSpec((CONV_W, LRU_HALF), lambda b, h, c: (0, h)),
        vec,
        pl.BlockSpec((LRU_BLOCKS // 2, LRU_BW, 2 * LRU_BW), lambda b, h, c: (h, 0, 0)),
        vec, vec, vec, vec,
    ]
    return pl.pallas_call(
        functools.partial(_lru_kernel, tc=tc),
        out_shape=(jax.ShapeDtypeStruct((batch * seq, D_LRU), BF16),
                   jax.ShapeDtypeStruct((batch, 8, D_LRU), F32)),
        grid=(batch, 2, nch),
        in_specs=in_specs,
        out_specs=(pl.BlockSpec((tc, LRU_HALF), lambda b, h, c: (b * nch + c, h)),
                   pl.BlockSpec((1, 8, LRU_HALF), lambda b, h, c: (b, 0, h))),
        scratch_shapes=[pltpu.VMEM((tc, LRU_HALF), F32), pltpu.VMEM((tc, LRU_HALF), F32),
                        pltpu.VMEM((8, LRU_HALF), F32)],
        compiler_params=_cparams(("arbitrary", "arbitrary", "arbitrary")),
        name="conv_rglru",
    )(proj, proj, hist0, proj, cw, cb, wax, ba, bx, lam, h0)


def _merge_kernel(o_ref, y_ref, wa_ref, wl_ref, ga_ref, gl_ref, z_ref):
    ya = jnp.dot(o_ref[...], wa_ref[...], preferred_element_type=F32)
    yl = jnp.dot(y_ref[...], wl_ref[...], preferred_element_type=F32)
    z = jax.nn.sigmoid(ga_ref[...].astype(F32)) * ya + jax.nn.sigmoid(gl_ref[...].astype(F32)) * yl
    z_ref[...] = z.astype(BF16)


def _merge(o_attn, y_lru, wa, wl, proj, tm=512, tn=1024):
    n_tok = o_attn.shape[0]
    ga_blk = GA_COL // tn
    gl_blk = GL_COL // tn
    return pl.pallas_call(
        _merge_kernel,
        out_shape=jax.ShapeDtypeStruct((n_tok, D_MODEL), BF16),
        grid=(D_MODEL // tn, n_tok // tm),
        in_specs=[pl.BlockSpec((tm, D_MODEL), lambda j, i: (i, 0)),
                  pl.BlockSpec((tm, D_MODEL), lambda j, i: (i, 0)),
                  pl.BlockSpec((D_MODEL, tn), lambda j, i: (0, j)),
                  pl.BlockSpec((D_MODEL, tn), lambda j, i: (0, j)),
                  pl.BlockSpec((tm, tn), lambda j, i: (i, ga_blk + j)),
                  pl.BlockSpec((tm, tn), lambda j, i: (i, gl_blk + j))],
        out_specs=pl.BlockSpec((tm, tn), lambda j, i: (i, j)),
        compiler_params=_cparams(("parallel", "parallel")),
        name="gated_merge",
    )(o_attn, y_lru, wa, wl, proj, proj)


def _layernorm(v, g, b):
    mu = jnp.mean(v, axis=-1, keepdims=True)
    dv = v - mu
    var = jnp.mean(dv * dv, axis=-1, keepdims=True)
    return dv * lax.rsqrt(var + LN_EPS) * g + b


def _ln1_kernel(x_ref, z_ref, w_ref, g_ref, b_ref, h_ref, hb_ref):
    mix = jnp.dot(z_ref[...], w_ref[...], preferred_element_type=F32)
    h = _layernorm(DN_ALPHA * x_ref[...] + mix, g_ref[...], b_ref[...])
    h_ref[...] = h
    hb_ref[...] = h.astype(BF16)


def _ln1(x2, z, w_out, g, b, tm=256):
    n_tok = x2.shape[0]
    row = lambda i: (i, 0)
    fixed = lambda i: (0, 0)
    return pl.pallas_call(
        _ln1_kernel,
        out_shape=(jax.ShapeDtypeStruct((n_tok, D_MODEL), F32),
                   jax.ShapeDtypeStruct((n_tok, D_MODEL), BF16)),
        grid=(n_tok // tm,),
        in_specs=[pl.BlockSpec((tm, D_MODEL), row), pl.BlockSpec((tm, D_MODEL), row),
                  pl.BlockSpec((D_MODEL, D_MODEL), fixed),
                  pl.BlockSpec((1, D_MODEL), fixed), pl.BlockSpec((1, D_MODEL), fixed)],
        out_specs=(pl.BlockSpec((tm, D_MODEL), row), pl.BlockSpec((tm, D_MODEL), row)),
        compiler_params=_cparams(("parallel",)),
        name="outproj_ln1",
    )(x2, z, w_out, g, b)


def _router_kernel(h_ref, w_ref, bias_ref, ti_ref, tw_ref, *, tm):
    logits = lax.dot_general(w_ref[...], h_ref[...], (((1,), (1,)), ((), ())), preferred_element_type=F32)
    scores = jax.nn.sigmoid(logits)
    sel = scores + bias_ref[...]
    neg = -jnp.inf
    e_iota = lax.broadcasted_iota(I32, (N_EXPERTS, tm), 0).astype(F32)
    g_iota32 = lax.broadcasted_iota(I32, (GROUP_SIZE, tm), 0).astype(F32)
    gs_rows = []
    for g in range(N_GROUPS):
        blk = sel[g * GROUP_SIZE:(g + 1) * GROUP_SIZE]
        m1 = jnp.max(blk, axis=0, keepdims=True)
        i1 = jnp.min(jnp.where(blk == m1, g_iota32, float(GROUP_SIZE)), axis=0, keepdims=True)
        m2 = jnp.max(jnp.where(g_iota32 == i1, neg, blk), axis=0, keepdims=True)
        gs_rows.append(m1 + m2)
    gs = jnp.concatenate(gs_rows, axis=0)
    g_iota = lax.broadcasted_iota(I32, (N_GROUPS, tm), 0).astype(F32)
    gsel = jnp.zeros((N_GROUPS, tm), F32)
    for _ in range(TOPK_GROUPS):
        m = jnp.max(gs, axis=0, keepdims=True)
        idx = jnp.min(jnp.where(gs == m, g_iota, float(N_GROUPS)), axis=0, keepdims=True)
        hit = g_iota == idx
        gsel = jnp.where(hit, 1.0, gsel)
        gs = jnp.where(hit, neg, gs)
    masked = jnp.concatenate(
        [jnp.where(gsel[g:g + 1] > 0.0, sel[g * GROUP_SIZE:(g + 1) * GROUP_SIZE], neg) for g in range(N_GROUPS)],
        axis=0)
    tis, tws = [], []
    for _ in range(TOP_K):
        m = jnp.max(masked, axis=0, keepdims=True)
        idx = jnp.min(jnp.where(masked == m, e_iota, float(N_EXPERTS)), axis=0, keepdims=True)
        hit = e_iota == idx
        tis.append(idx)
        tws.append(jnp.sum(jnp.where(hit, scores, 0.0), axis=0, keepdims=True))
        masked = jnp.where(hit, neg, masked)
    tw = jnp.concatenate(tws, axis=0)
    ti_ref[...] = jnp.concatenate(tis, axis=0).astype(I32)
    tw_ref[...] = tw / jnp.sum(tw, axis=0, keepdims=True) * ROUTED_SCALE


def _router(hb, wr_t, bias, tm=512):
    n_tok = hb.shape[0]
    return pl.pallas_call(
        functools.partial(_router_kernel, tm=tm),
        out_shape=(jax.ShapeDtypeStruct((TOP_K, n_tok), I32), jax.ShapeDtypeStruct((TOP_K, n_tok), F32)),
        grid=(n_tok // tm,),
        in_specs=[pl.BlockSpec((tm, D_MODEL), lambda i: (i, 0)),
                  pl.BlockSpec((N_EXPERTS, D_MODEL), lambda i: (0, 0)),
                  pl.BlockSpec((N_EXPERTS, 1), lambda i: (0, 0))],
        out_specs=(pl.BlockSpec((TOP_K, tm), lambda i: (0, i)), pl.BlockSpec((TOP_K, tm), lambda i: (0, i))),
        compiler_params=_cparams(("parallel",)),
        name="router_topk",
    )(hb, wr_t, bias)


def _expert_kernel(nblk_ref, bst_ref, xs_ref, wg_ref, wu_ref, wd_ref, ys_ref,
                   xbuf, obuf, isem, osem, wgb, wub, wdb):
    e = pl.program_id(0)
    nb = nblk_ref[e]
    b0 = bst_ref[e]
    nused = bst_ref[N_EXPERTS - 1] + nblk_ref[N_EXPERTS - 1]
    ahead = EXPERT_RING - 1

    def rows(g):
        return pl.ds(pl.multiple_of(g * EXPERT_BLOCK, EXPERT_BLOCK), EXPERT_BLOCK)

    def in_copy(g):
        slot = g & (EXPERT_RING - 1)
        return pltpu.make_async_copy(xs_ref.at[rows(g), :], xbuf.at[slot], isem.at[slot])

    def out_copy(g):
        slot = g & (EXPERT_RING - 1)
        return pltpu.make_async_copy(obuf.at[slot], ys_ref.at[rows(g), :], osem.at[slot])

    @pl.when(nb > 0)
    def _():
        @pl.when(b0 == 0)
        def _():
            for q in range(ahead):
                @pl.when(q < nused)
                def _(q=q):
                    in_copy(q).start(priority=1)

        wgb[...] = wg_ref[0].astype(BF16)
        wub[...] = wu_ref[0].astype(BF16)
        wdb[...] = wd_ref[0].astype(BF16)

        def body(j, carry):
            g = b0 + j
            slot = g & (EXPERT_RING - 1)

            @pl.when(g + ahead < nused)
            def _():
                in_copy(g + ahead).start(priority=1)

            in_copy(g).wait()
            x = xbuf[slot].astype(BF16)
            hid = jax.nn.silu(jnp.dot(x, wgb[...], preferred_element_type=F32)) * jnp.dot(
                x, wub[...], preferred_element_type=F32)
            out = jnp.dot(hid.astype(BF16), wdb[...], preferred_element_type=F32)

            @pl.when(g >= EXPERT_RING)
            def _():
                out_copy(g - EXPERT_RING).wait()

            obuf[slot] = out
            out_copy(g).start(priority=1)
            return carry

        lax.fori_loop(0, nb, body, 0)

    @pl.when(e == N_EXPERTS - 1)
    def _():
        for q in range(EXPERT_RING, 0, -1):
            @pl.when(nused >= q)
            def _(q=q):
                out_copy(nused - q).wait()


def _experts(nblk, bst, xs, wg, wu, wd):
    grid_spec = pltpu.PrefetchScalarGridSpec(
        num_scalar_prefetch=2, grid=(N_EXPERTS,),
        in_specs=[pl.BlockSpec(memory_space=pl.ANY),
                  pl.BlockSpec((1, D_MODEL, D_EXPERT), lambda e, nb, bs: (e, 0, 0)),
                  pl.BlockSpec((1, D_MODEL, D_EXPERT), lambda e, nb, bs: (e, 0, 0)),
                  pl.BlockSpec((1, D_EXPERT, D_MODEL), lambda e, nb, bs: (e, 0, 0))],
        out_specs=pl.BlockSpec(memory_space=pl.ANY),
        scratch_shapes=[pltpu.VMEM((EXPERT_RING, EXPERT_BLOCK, D_MODEL), F32),
                        pltpu.VMEM((EXPERT_RING, EXPERT_BLOCK, D_MODEL), F32),
                        pltpu.SemaphoreType.DMA((EXPERT_RING,)), pltpu.SemaphoreType.DMA((EXPERT_RING,)),
                        pltpu.VMEM((D_MODEL, D_EXPERT), BF16), pltpu.VMEM((D_MODEL, D_EXPERT), BF16),
                        pltpu.VMEM((D_EXPERT, D_MODEL), BF16)])
    return pl.pallas_call(
        _expert_kernel,
        out_shape=jax.ShapeDtypeStruct(xs.shape, F32),
        grid_spec=grid_spec,
        input_output_aliases={2: 0},
        compiler_params=_cparams(("arbitrary",), EXPERT_VMEM_LIMIT),
        name="routed_experts",
    )(nblk, bst, xs, wg, wu, wd)


COMBINE_CHUNK = 16


def _combine_kernel(dc_ref, dn_ref, ys_ref, tw_ref, h_ref, sh_ref, g_ref, b_ref, o_ref, gbuf, sem, *, tm):
    i = pl.program_id(0)
    nsteps = pl.num_programs(0)
    nchunks = tm // COMBINE_CHUNK

    def issue_tile(d_ref, s):
        def body(t, carry):
            for k in range(TOP_K):
                pltpu.make_async_copy(ys_ref.at[pl.ds(d_ref[0, k, t], 1), :],
                                      gbuf.at[s, k, pl.ds(t, 1), :], sem.at[s]).start(priority=k % 2)
            return carry
        lax.fori_loop(0, tm, body, 0, unroll=2)

    def wait_slabs(s):
        for k in range(TOP_K):
            pltpu.make_async_copy(ys_ref.at[pl.ds(0, tm), :], gbuf.at[s, k], sem.at[s]).wait()

    @pl.when(i == 0)
    def _():
        issue_tile(dc_ref, 0)

    g = g_ref[...]
    b = b_ref[...]

    def reduce_tile(s):
        def chunk(c, carry):
            rs = pl.ds(pl.multiple_of(c * COMBINE_CHUNK, COMBINE_CHUNK), COMBINE_CHUNK)
            tw = tw_ref[rs, :]
            routed = tw[:, 0:1] * gbuf[s, 0, rs, :]
            for k in range(1, TOP_K):
                routed = routed + tw[:, k:k + 1] * gbuf[s, k, rs, :]
            v = DN_ALPHA * h_ref[rs, :] + (routed + sh_ref[rs, :].astype(F32))
            o_ref[rs, :] = _layernorm(v, g, b)
            return carry
        lax.fori_loop(0, nchunks, chunk, 0)

    for s in (0, 1):
        @pl.when((i & 1) == s)
        def _(s=s):
            @pl.when(i + 1 < nsteps)
            def _():
                issue_tile(dn_ref, 1 - s)

            wait_slabs(s)
            reduce_tile(s)


def _combine(dest3, ys, topw, h1, shared, g, b):
    n_tok = h1.shape[0]
    nsteps, _, tm = dest3.shape
    row = lambda i: (i, 0)
    fixed = lambda i: (0, 0)
    return pl.pallas_call(
        functools.partial(_combine_kernel, tm=tm),
        out_shape=jax.ShapeDtypeStruct((n_tok, D_MODEL), F32),
        grid=(nsteps,),
        in_specs=[pl.BlockSpec((1, TOP_K, tm), lambda i: (i, 0, 0), memory_space=pltpu.SMEM),
                  pl.BlockSpec((1, TOP_K, tm), lambda i: (jnp.minimum(i + 1, nsteps - 1), 0, 0),
                               memory_space=pltpu.SMEM),
                  pl.BlockSpec(memory_space=pl.ANY),
                  pl.BlockSpec((tm, TOP_K), row),
                  pl.BlockSpec((tm, D_MODEL), row), pl.BlockSpec((tm, D_MODEL), row),
                  pl.BlockSpec((1, D_MODEL), fixed), pl.BlockSpec((1, D_MODEL), fixed)],
        out_specs=pl.BlockSpec((tm, D_MODEL), row),
        scratch_shapes=[pltpu.VMEM((2, TOP_K, tm, D_MODEL), F32), pltpu.SemaphoreType.DMA((2,))],
        compiler_params=_cparams(("arbitrary",)),
        name="combine_ln2",
    )(dest3, dest3, ys, topw, h1, shared, g, b)


PLAN_TM = 128
LANES = 128
SUBLANES = 8


def _small_int_halves(v):
    hi = jnp.floor(v * (1.0 / 256.0))
    return hi.astype(BF16), (v - hi * 256.0).astype(BF16)


def _plan_kernel(ti_ref, dest_ref, info_ref, rank_s, cntc_s, cntr_s, pst_s, *, tm):
    p = pl.program_id(0)
    i = pl.program_id(1)
    ti = ti_ref[...]
    e_iota = lax.broadcasted_iota(I32, (N_EXPERTS, tm), 0)
    reps = tm // LANES

    @pl.when(jnp.logical_and(p == 0, i == 0))
    def _():
        cntc_s[...] = jnp.zeros(cntc_s.shape, F32)
        cntr_s[...] = jnp.zeros(cntr_s.shape, F32)

    @pl.when(p == 0)
    def _():
        hits = [e_iota == ti[k:k + 1] for k in range(TOP_K)]
        m = jnp.where(hits[0], 1.0, 0.0)
        for k in range(1, TOP_K):
            m = m + jnp.where(hits[k], 1.0, 0.0)
        mb = m.astype(BF16)
        earlier = lax.broadcasted_iota(I32, (tm, tm), 0) < lax.broadcasted_iota(I32, (tm, tm), 1)
        pfx = jnp.dot(mb, jnp.where(earlier, 1.0, 0.0).astype(BF16), preferred_element_type=F32)
        val = pfx + jnp.concatenate([cntc_s[...]] * reps, axis=1)
        rank_s[i] = jnp.concatenate(
            [jnp.sum(jnp.where(hits[k], val, 0.0), axis=0, keepdims=True) for k in range(TOP_K)], axis=0)
        cntc_s[...] += jnp.dot(mb, jnp.ones((tm, LANES), BF16), preferred_element_type=F32)
        cntr_s[...] += lax.dot_general(jnp.ones((8, tm), BF16), mb, (((1,), (1,)), ((), ())),
                                       preferred_element_type=F32)

    @pl.when(jnp.logical_and(p == 1, i == 0))
    def _():
        def n_blocks_of(cnt):
            return jnp.right_shift(cnt.astype(I32) + (EXPERT_BLOCK - 1), EXPERT_BLOCK.bit_length() - 1).astype(F32)

        ee0 = lax.broadcasted_iota(I32, (N_EXPERTS, N_EXPERTS), 0)
        ee1 = lax.broadcasted_iota(I32, (N_EXPERTS, N_EXPERTS), 1)
        hi, lo = _small_int_halves(n_blocks_of(cntc_s[...]))
        lower = jnp.where(ee1 < ee0, 1.0, 0.0).astype(BF16)
        bst_c = 256.0 * jnp.dot(lower, hi, preferred_element_type=F32) + jnp.dot(lower, lo, preferred_element_type=F32)
        pst_s[...] = bst_c * float(EXPERT_BLOCK)
        cnt_r = cntr_s[...]
        nb_r = n_blocks_of(cnt_r)
        hi, lo = _small_int_halves(nb_r)
        upper = jnp.where(ee0 < ee1, 1.0, 0.0).astype(BF16)
        bst_r = 256.0 * jnp.dot(hi, upper, preferred_element_type=F32) + jnp.dot(lo, upper, preferred_element_type=F32)
        sel = lax.broadcasted_iota(I32, (8, N_EXPERTS), 0)
        info = jnp.where(sel == 0, cnt_r, jnp.where(sel == 1, nb_r, jnp.where(sel == 2, bst_r, 0.0)))
        info_ref[...] = info.astype(I32)

    @pl.when(p == 1)
    def _():
        pst = jnp.concatenate([pst_s[...]] * reps, axis=1)
        off = jnp.concatenate(
            [jnp.sum(jnp.where(e_iota == ti[k:k + 1], pst, 0.0), axis=0, keepdims=True) for k in range(TOP_K)], axis=0)
        dest_ref[0] = (rank_s[i] + off).astype(I32)


def _plan(topi_t, tm=PLAN_TM):
    n_tok = topi_t.shape[1]
    nt = n_tok // tm
    return pl.pallas_call(
        functools.partial(_plan_kernel, tm=tm),
        out_shape=(jax.ShapeDtypeStruct((nt, TOP_K, tm), I32), jax.ShapeDtypeStruct((8, N_EXPERTS), I32)),
        grid=(2, nt),
        in_specs=[pl.BlockSpec((TOP_K, tm), lambda p, i: (0, i))],
        out_specs=(pl.BlockSpec((1, TOP_K, tm), lambda p, i: (i * p, 0, 0)),
                   pl.BlockSpec((8, N_EXPERTS), lambda p, i: (0, 0))),
        scratch_shapes=[pltpu.VMEM((nt, TOP_K, tm), F32), pltpu.VMEM((N_EXPERTS, LANES), F32),
                        pltpu.VMEM((8, N_EXPERTS), F32), pltpu.VMEM((N_EXPERTS, LANES), F32)],
        compiler_params=_cparams(("arbitrary", "arbitrary")),
        name="expert_row_plan",
    )(topi_t)


def _dispatch_kernel(cnt_ref, nblk_ref, bst_ref, dest_ref, hp_ref, hb_ref, wg_ref, wu_ref, wd_ref, xs_ref, sh_ref,
                     zrow, sem, zsem, gsem, tsem, *, tm, epg, tpg, n_blocks):
    i = pl.program_id(0)

    @pl.when(i == 0)
    def _():
        zrow[...] = jnp.zeros(zrow.shape, F32)

    nused = bst_ref[N_EXPERTS - 1] + nblk_ref[N_EXPERTS - 1]
    t_lo = jnp.minimum(nused + i * tpg, n_blocks)
    t_hi = jnp.minimum(nused + (i + 1) * tpg, n_blocks)

    def tail_copy(blk):
        return pltpu.make_async_copy(
            zrow, xs_ref.at[pl.ds(pl.multiple_of(blk * EXPERT_BLOCK, EXPERT_BLOCK), EXPERT_BLOCK), :], tsem)

    def tfill(blk, carry):
        tail_copy(blk).start()
        return carry

    def twait(blk, carry):
        tail_copy(blk).wait()
        return carry

    lax.fori_loop(t_lo, t_hi, tfill, 0)

    def scatter(t, carry):
        for k in range(TOP_K):
            pltpu.make_async_copy(hp_ref.at[pl.ds(t, 1), :], xs_ref.at[pl.ds(dest_ref[0, k, t], 1), :],
                                  sem).start(priority=k % 2)
        return carry

    lax.fori_loop(0, tm, scatter, 0, unroll=2)

    def zfill(r, carry):
        pltpu.make_async_copy(zrow.at[pl.ds(0, 1), :], xs_ref.at[pl.ds(r, 1), :], zsem).start()
        return carry

    def zwait(r, carry):
        pltpu.make_async_copy(zrow.at[pl.ds(0, 1), :], xs_ref.at[pl.ds(0, 1), :], zsem).wait()
        return carry

    def group_copy(grp):
        return pltpu.make_async_copy(
            zrow.at[pl.ds(0, SUBLANES), :], xs_ref.at[pl.ds(pl.multiple_of(grp * SUBLANES, SUBLANES), SUBLANES), :],
            gsem)

    def gfill(grp, carry):
        group_copy(grp).start()
        return carry

    def gwait(grp, carry):
        group_copy(grp).wait()
        return carry

    bounds = []
    for q in range(epg):
        e = i * epg + q
        lo = bst_ref[e] * EXPERT_BLOCK + cnt_ref[e]
        hi = (bst_ref[e] + nblk_ref[e]) * EXPERT_BLOCK
        mid = jnp.minimum(jnp.bitwise_and(lo + (SUBLANES - 1), -SUBLANES), hi)
        g_lo = jnp.right_shift(mid, SUBLANES.bit_length() - 1)
        g_hi = jnp.right_shift(hi, SUBLANES.bit_length() - 1)
        bounds.append((lo, mid, g_lo, g_hi))
        lax.fori_loop(lo, mid, zfill, 0)
        lax.fori_loop(g_lo, g_hi, gfill, 0)

    hb = hb_ref[...]
    hid = jax.nn.silu(jnp.dot(hb, wg_ref[...], preferred_element_type=F32)) * jnp.dot(
        hb, wu_ref[...], preferred_element_type=F32)
    sh_ref[...] = jnp.dot(hid.astype(BF16), wd_ref[...], preferred_element_type=F32).astype(BF16)

    for k in range(TOP_K):
        pltpu.make_async_copy(hp_ref, xs_ref.at[pl.ds(0, tm), :], sem).wait()
    for lo, mid, g_lo, g_hi in bounds:
        lax.fori_loop(lo, mid, zwait, 0)
        lax.fori_loop(g_lo, g_hi, gwait, 0)
    lax.fori_loop(t_lo, t_hi, twait, 0)


def _dispatch(cnt, nblk, bst, dest3, hp, hb, wgs, wus, wds, n_rows):
    nsteps, _, tm = dest3.shape
    n_tok = hp.shape[0]
    n_blocks = n_rows // EXPERT_BLOCK
    assert N_EXPERTS % nsteps == 0
    row = lambda i, *_: (i, 0)
    fixed = lambda i, *_: (0, 0)
    grid_spec = pltpu.PrefetchScalarGridSpec(
        num_scalar_prefetch=3, grid=(nsteps,),
        in_specs=[pl.BlockSpec((1, TOP_K, tm), lambda i, *_: (i, 0, 0), memory_space=pltpu.SMEM),
                  pl.BlockSpec((tm, D_MODEL), row), pl.BlockSpec((tm, D_MODEL), row),
                  pl.BlockSpec((D_MODEL, D_EXPERT), fixed), pl.BlockSpec((D_MODEL, D_EXPERT), fixed),
                  pl.BlockSpec((D_EXPERT, D_MODEL), fixed)],
        out_specs=(pl.BlockSpec(memory_space=pl.ANY), pl.BlockSpec((tm, D_MODEL), row)),
        scratch_shapes=[pltpu.VMEM((EXPERT_BLOCK, D_MODEL), F32), pltpu.SemaphoreType.DMA(()),
                        pltpu.SemaphoreType.DMA(()), pltpu.SemaphoreType.DMA(()), pltpu.SemaphoreType.DMA(())])
    return pl.pallas_call(
        functools.partial(_dispatch_kernel, tm=tm, epg=N_EXPERTS // nsteps, tpg=-(-n_blocks // nsteps),
                          n_blocks=n_blocks),
        out_shape=(jax.ShapeDtypeStruct((n_rows, D_MODEL), F32), jax.ShapeDtypeStruct((n_tok, D_MODEL), BF16)),
        grid_spec=grid_spec,
        compiler_params=_cparams(("arbitrary",)),
        name="dispatch_shared",
    )(cnt, nblk, bst, dest3, hp, hb, wgs, wus, wds)


def _rope_tables(n_pos):
    half = ROT_DIM // 2
    inv_freq = ROPE_THETA ** (-jnp.arange(half, dtype=F32) / half)
    ang = jnp.arange(n_pos).astype(F32)[:, None] * inv_freq[None, :]
    cos, sin = jnp.cos(ang), jnp.sin(ang)
    c = jnp.concatenate([cos, cos, jnp.ones((n_pos, HEAD_DIM - ROT_DIM), F32)], axis=1)
    s = jnp.concatenate([sin, sin, jnp.zeros((n_pos, HEAD_DIM - ROT_DIM), F32)], axis=1)
    return c, s


def kernel(x, meta_tokens, w_in, conv_w, conv_b, w_rg_a, b_rg_a, w_rg_x, b_rg_x, rg_lambda, attn_sinks, w_o_attn, w_o_lru, w_out, ln1_g, ln1_b, w_router, router_bias, w_gate_e, w_up_e, w_down_e, w_gate_s, w_up_s, w_down_s, ln2_g, ln2_b):
    batch, seq, _ = x.shape
    n_tok = batch * seq
    assert w_in.shape[0] == 1 and seq % 512 == 0 and meta_tokens.shape[0] == N_META
    x2 = x.reshape(n_tok, D_MODEL)
    row = lambda v: v.reshape(1, -1).astype(F32)

    wax = jnp.concatenate([w_rg_a[0], w_rg_x[0]], axis=-1).astype(BF16)
    c_t, s_t = _rope_tables(N_META + seq)
    tabs_meta = (c_t[:N_META], s_t[:N_META])
    tabs_real = (c_t[N_META:], s_t[N_META:])

    projm = _matmul(meta_tokens, w_in[0], BF16, N_META, 1024)
    lru_args = (conv_w[0], row(conv_b[0]), wax, row(b_rg_a[0]), row(b_rg_x[0]), row(rg_lambda[0]))
    _, h_meta = _lru(projm, jnp.zeros((N_META, D_LRU), BF16), jnp.zeros((1, D_LRU), F32), *lru_args,
                     batch=1, seq=N_META, tc=N_META)

    proj = _matmul(x2, w_in[0], BF16, 1024, 1024)
    o_attn = _attention(proj, projm, attn_sinks[0].astype(F32), tabs_real, tabs_meta, batch, seq)
    y_lru, _ = _lru(proj, projm[:, XR_COL:XR_COL + D_LRU], h_meta[0, 7:8], *lru_args, batch=batch, seq=seq, tc=LRU_SUB)
    z = _merge(o_attn, y_lru, w_o_attn[0].astype(BF16), w_o_lru[0].astype(BF16), proj)
    h1, h1b = _ln1(x2, z, w_out[0].astype(BF16), row(ln1_g[0]), row(ln1_b[0]))

    topi_t, topw_t = _router(h1b, w_router[0].T.astype(BF16), router_bias[0].reshape(N_EXPERTS, 1).astype(F32))
    dest3, info = _plan(topi_t)
    cnt, nblk, bst = info[0], info[1], info[2]
    n_rows = -(-(n_tok * TOP_K + N_EXPERTS * (EXPERT_BLOCK - 1)) // EXPERT_BLOCK) * EXPERT_BLOCK
    xs, shared = _dispatch(cnt, nblk, bst, dest3, h1, h1b,w_gate_s[0].astype(BF16), w_up_s[0].astype(BF16),
                           w_down_s[0].astype(BF16), n_rows)
    ys = _experts(nblk, bst, xs, w_gate_e[0], w_up_e[0], w_down_e[0])
    out = _combine(dest3, ys, topw_t.T, h1, shared, row(ln2_g[0]), row(ln2_b[0]))
    return out.reshape(batch, seq, D_MODEL)
```

```python
import functools

import jax
import jax.numpy as jnp
from jax import lax
from jax.experimental import pallas as pl
from jax.experimental.pallas import tpu as pltpu

F32 = jnp.float32
BF16 = jnp.bfloat16
U32 = jnp.uint32
I32 = jnp.int32

D_MODEL = 2048
N_META = 16
N_Q_HEADS = 16
N_KV_HEADS = 4
HEAD_DIM = 128
Q_GROUP = N_Q_HEADS // N_KV_HEADS
ROT_DIM = HEAD_DIM // 4
ROPE_THETA = 500000.0
ATTN_BLOCK = 128
D_LRU = D_MODEL
LRU_BLOCKS = 16
LRU_BW = D_LRU // LRU_BLOCKS
CONV_W = 4
RGLRU_C = 8.0
N_EXPERTS = 256
TOP_K = 8
N_GROUPS = 8
GROUP_SIZE = N_EXPERTS // N_GROUPS
TOPK_GROUPS = 4
D_EXPERT = 512
ROUTED_SCALE = 2.5
EXPERT_BLOCK = 128
EXPERT_RING = 8
DN_ALPHA = 2.0 ** 0.25
LN_EPS = 1e-5
NEG_INF = -1e30
KV_DIM = N_KV_HEADS * HEAD_DIM
Q_DIM = N_Q_HEADS * HEAD_DIM
K_COLBLK = Q_DIM // KV_DIM
V_COLBLK = K_COLBLK + 1
XR_COL = Q_DIM + 2 * KV_DIM
XG_COL = XR_COL + D_LRU
GA_COL = XG_COL + D_LRU
GL_COL = GA_COL + D_MODEL
LRU_HALF = D_LRU // 2
LRU_SUB = 256
VMEM_LIMIT = 50 * 1024 * 1024
EXPERT_VMEM_LIMIT = (2 * 3 * D_MODEL * D_EXPERT * 4 + 3 * D_MODEL * D_EXPERT * 2
                     + 2 * EXPERT_RING * EXPERT_BLOCK * D_MODEL * 4 + 8 * 1024 * 1024)


def _cparams(sem, vmem_limit=VMEM_LIMIT):
    return pltpu.CompilerParams(dimension_semantics=sem, vmem_limit_bytes=vmem_limit)


def _mm_kernel(a_ref, b_ref, o_ref, b_bf):
    @pl.when(pl.program_id(1) == 0)
    def _():
        b_bf[...] = b_ref[...].astype(BF16)

    o_ref[...] = jnp.dot(a_ref[...].astype(BF16), b_bf[...], preferred_element_type=F32).astype(o_ref.dtype)


def _matmul(a, b, out_dtype, tm, tn):
    m, k = a.shape
    n = b.shape[1]
    return pl.pallas_call(
        _mm_kernel,
        out_shape=jax.ShapeDtypeStruct((m, n), out_dtype),
        grid=(n // tn, m // tm),
        in_specs=[pl.BlockSpec((tm, k), lambda j, i: (i, 0)),
                  pl.BlockSpec((k, tn), lambda j, i: (0, j))],
        out_specs=pl.BlockSpec((tm, tn), lambda j, i: (i, j)),
        scratch_shapes=[pltpu.VMEM((k, tn), BF16)],
        compiler_params=_cparams(("arbitrary", "arbitrary")),
        name="dense_matmul",
    )(a, b)


def _rope(x, rot, c, s):
    return x.astype(F32) * c + jnp.dot(x, rot, preferred_element_type=F32) * s


def _attn_kernel(sink_ref, q_ref, kc_ref, kp_ref, vc_ref, vp_ref, km_ref, vm_ref,
                 cq_ref, sq_ref, cp_ref, sp_ref, cm_ref, sm_ref, o_ref, *, tq):
    n = pl.program_id(1)
    nsub = tq // ATTN_BLOCK
    scale = HEAD_DIM ** -0.5
    cq, sq = cq_ref[...], sq_ref[...]
    half = ROT_DIM // 2
    rj = lax.broadcasted_iota(I32, (HEAD_DIM, HEAD_DIM), 0)
    ri = lax.broadcasted_iota(I32, (HEAD_DIM, HEAD_DIM), 1)
    rot = (jnp.where(jnp.logical_and(ri < half, rj == ri + half), -1.0, 0.0)
           + jnp.where(jnp.logical_and(jnp.logical_and(ri >= half, ri < ROT_DIM), rj == ri - half), 1.0, 0.0)
           ).astype(BF16)
    rows = Q_GROUP * ATTN_BLOCK
    r_idx = lax.broadcasted_iota(I32, (rows, ATTN_BLOCK), 0) % ATTN_BLOCK
    c_idx = lax.broadcasted_iota(I32, (rows, ATTN_BLOCK), 1)
    cur_ok = c_idx <= r_idx
    prev_ok = c_idx > r_idx
    first_prev_ok = (c_idx - r_idx) > jnp.where(n > 0, 0, ATTN_BLOCK)
    meta_ok = c_idx < N_META
    pad_rows = jnp.zeros((ATTN_BLOCK - N_META, HEAD_DIM), BF16)
    ones_blk = jnp.ones((3 * ATTN_BLOCK, HEAD_DIM), BF16)
    dn = (((1,), (1,)), ((), ()))
    for g in range(N_KV_HEADS):
        hs = slice(g * HEAD_DIM, (g + 1) * HEAD_DIM)
        k_g = _rope(kc_ref[:, hs], rot, cq, sq).astype(BF16)
        kp_g = _rope(kp_ref[:, hs], rot, cp_ref[...], sp_ref[...]).astype(BF16)
        km_g = jnp.concatenate([_rope(km_ref[:, hs], rot, cm_ref[...], sm_ref[...]).astype(BF16), pad_rows], axis=0)
        v_g = vc_ref[:, hs]
        vp_g = vp_ref[:, hs]
        vm_g = jnp.concatenate([vm_ref[:, hs], pad_rows], axis=0)
        sink = jnp.concatenate(
            [jnp.full((ATTN_BLOCK, 1), sink_ref[g * Q_GROUP + j], F32) for j in range(Q_GROUP)], axis=0)
        for sub in range(nsub):
            rs = slice(sub * ATTN_BLOCK, (sub + 1) * ATTN_BLOCK)
            q_st = jnp.concatenate(
                [_rope(q_ref[rs, (g * Q_GROUP + j) * HEAD_DIM:(g * Q_GROUP + j + 1) * HEAD_DIM], rot,
                       cq[rs], sq[rs]) for j in range(Q_GROUP)], axis=0).astype(BF16)
            if sub == 0:
                k_all = jnp.concatenate([kp_g, k_g[rs], km_g], axis=0)
                v_all = jnp.concatenate([vp_g, v_g[rs], vm_g], axis=0)
                p_ok = first_prev_ok
            else:
                both = slice((sub - 1) * ATTN_BLOCK, (sub + 1) * ATTN_BLOCK)
                k_all = jnp.concatenate([k_g[both], km_g], axis=0)
                v_all = jnp.concatenate([v_g[both], vm_g], axis=0)
                p_ok = prev_ok
            sc = lax.dot_general(q_st, k_all, dn, preferred_element_type=F32) * scale
            s_p = jnp.where(p_ok, sc[:, :ATTN_BLOCK], NEG_INF)
            s_c = jnp.where(cur_ok, sc[:, ATTN_BLOCK:2 * ATTN_BLOCK], NEG_INF)
            s_m = jnp.where(meta_ok, sc[:, 2 * ATTN_BLOCK:], NEG_INF)
            m = jnp.maximum(jnp.max(jnp.maximum(jnp.maximum(s_p, s_c), s_m), axis=1, keepdims=True), sink)
            p = jnp.concatenate([jnp.exp(s_p - m), jnp.exp(s_c - m), jnp.exp(s_m - m)], axis=1).astype(BF16)
            od = jnp.dot(p, jnp.concatenate([v_all, ones_blk], axis=1), preferred_element_type=F32)
            o = od[:, :HEAD_DIM] / (od[:, HEAD_DIM:] + jnp.exp(sink - m))
            for j in range(Q_GROUP):
                h = g * Q_GROUP + j
                o_ref[rs, h * HEAD_DIM:(h + 1) * HEAD_DIM] = o[j * ATTN_BLOCK:(j + 1) * ATTN_BLOCK].astype(BF16)


def _attention(proj, projm, sinks, tabs_real, tabs_meta, batch, seq, tq=512):
    n_tok = batch * seq
    nq = seq // tq
    sub_per = tq // ATTN_BLOCK

    def prev_blk(b, n):
        return jnp.maximum(b * (seq // ATTN_BLOCK) + n * sub_per - 1, 0)

    tab_spec = pl.BlockSpec((tq, HEAD_DIM), lambda b, n, s: (n, 0))
    tabp_spec = pl.BlockSpec((ATTN_BLOCK, HEAD_DIM), lambda b, n, s: (jnp.maximum(n * sub_per - 1, 0), 0))
    tabm_spec = pl.BlockSpec((N_META, HEAD_DIM), lambda b, n, s: (0, 0))
    in_specs = [
        pl.BlockSpec((tq, D_MODEL), lambda b, n, s: (b * nq + n, 0)),
        pl.BlockSpec((tq, KV_DIM), lambda b, n, s: (b * nq + n, K_COLBLK)),
        pl.BlockSpec((ATTN_BLOCK, KV_DIM), lambda b, n, s: (prev_blk(b, n), K_COLBLK)),
        pl.BlockSpec((tq, KV_DIM), lambda b, n, s: (b * nq + n, V_COLBLK)),
        pl.BlockSpec((ATTN_BLOCK, KV_DIM), lambda b, n, s: (prev_blk(b, n), V_COLBLK)),
        pl.BlockSpec((N_META, KV_DIM), lambda b, n, s: (0, K_COLBLK)),
        pl.BlockSpec((N_META, KV_DIM), lambda b, n, s: (0, V_COLBLK)),
        tab_spec, tab_spec, tabp_spec, tabp_spec, tabm_spec, tabm_spec,
    ]
    grid_spec = pltpu.PrefetchScalarGridSpec(
        num_scalar_prefetch=1, grid=(batch, nq), in_specs=in_specs,
        out_specs=pl.BlockSpec((tq, D_MODEL), lambda b, n, s: (b * nq + n, 0)))
    return pl.pallas_call(
        functools.partial(_attn_kernel, tq=tq),
        out_shape=jax.ShapeDtypeStruct((n_tok, D_MODEL), BF16),
        grid_spec=grid_spec,
        compiler_params=_cparams(("parallel", "parallel")),
        name="swa_attention",
    )(sinks, proj, proj, proj, proj, proj, projm, projm,
      tabs_real[0], tabs_real[1], tabs_real[0], tabs_real[1], tabs_meta[0], tabs_meta[1])


def _softplus(z):
    return jnp.maximum(z, 0.0) + jnp.log1p(jnp.exp(-jnp.abs(z)))


def _sigmoid(z):
    return 0.5 * jnp.tanh(0.5 * z) + 0.5


def _lru_kernel(xr_ref, prev_ref, hist0_ref, xg_ref, cw_ref, cb_ref, wax_ref, ba_ref, bx_ref, lam_ref, h0_ref,
                y_ref, hl_ref, a_s, u_s, hcar, *, tc):
    c = pl.program_id(2)
    width = a_s.shape[1]
    blocks = width // LRU_BW

    @pl.when(c == 0)
    def _():
        hcar[...] = jnp.broadcast_to(h0_ref[...], hcar.shape)

    first = c == 0
    sub = min(tc, LRU_SUB)

    def gate_rows(sc, carry):
        r0 = pl.multiple_of(sc * sub, sub)
        rs = pl.ds(r0, sub)
        before = pl.ds(pl.multiple_of(jnp.maximum(r0 - N_META, 0), N_META), N_META)
        for nb in range(blocks):
            cs = slice(nb * LRU_BW, (nb + 1) * LRU_BW)
            hist = jnp.where(sc > 0, xr_ref[before, cs], jnp.where(first, hist0_ref[:, cs], prev_ref[:, cs]))
            x = xr_ref[rs, cs].astype(F32)
            ext = jnp.concatenate([hist.astype(F32), x], axis=0)
            cw = cw_ref[:, cs]
            y = cb_ref[:, cs] + x * cw[CONV_W - 1:CONV_W]
            for d in range(1, CONV_W):
                y = y + pltpu.roll(ext, d, 0)[N_META:] * cw[CONV_W - 1 - d:CONV_W - d]
            gates = jnp.dot(y.astype(BF16), wax_ref[nb], preferred_element_type=F32)
            r = _sigmoid(gates[:, :LRU_BW] + ba_ref[:, cs])
            ig = _sigmoid(gates[:, LRU_BW:] + bx_ref[:, cs])
            log_a = r * ((-RGLRU_C) * _softplus(-lam_ref[:, cs]))
            a = jnp.exp(log_a)
            a_s[rs, cs] = a
            u_s[rs, cs] = jnp.sqrt(-jnp.tanh(log_a) * (a * a + 1.0)) * (ig * y)
        return carry

    lax.fori_loop(0, tc // sub, gate_rows, 0)
    row = lax.broadcasted_iota(I32, (8, width), 0)

    def body(i, h):
        sl = pl.ds(pl.multiple_of(i * 8, 8), 8)
        a = a_s[sl, :]
        u = u_s[sl, :]
        for d in (1, 2, 4):
            ok = row >= d
            u = jnp.where(ok, a * pltpu.roll(u, d, 0) + u, u)
            a = jnp.where(ok, a * pltpu.roll(a, d, 0), a)
        hs = a * h + u
        u_s[sl, :] = hs
        return jnp.broadcast_to(hs[7:8, :], (8, width))

    h_fin = lax.fori_loop(0, tc // 8, body, hcar[...])
    hcar[...] = h_fin
    hl_ref[0] = h_fin

    def gate_out(sc, carry):
        rs = pl.ds(pl.multiple_of(sc * sub, sub), sub)
        for nb in range(blocks):
            cs = slice(nb * LRU_BW, (nb + 1) * LRU_BW)
            y_ref[rs, cs] = (u_s[rs, cs] * jax.nn.gelu(xg_ref[rs, cs].astype(F32))).astype(BF16)
        return carry

    lax.fori_loop(0, tc // sub, gate_out, 0)


def _lru(proj, hist0, h0, cw, cb, wax, ba, bx, lam, batch, seq, tc):
    nch = seq // tc
    per16 = tc // N_META

    xr_blk = XR_COL // LRU_HALF
    xg_blk = XG_COL // LRU_HALF

    def prev_idx(b, h, c):
        return (jnp.maximum(b * (seq // N_META) + c * per16 - 1, 0), xr_blk + h)

    vec = pl.BlockSpec((1, LRU_HALF), lambda b, h, c: (0, h))
    in_specs = [
        pl.BlockSpec((tc, LRU_HALF), lambda b, h, c: (b * nch + c, xr_blk + h)),
        pl.BlockSpec((N_META, LRU_HALF), prev_idx),
        pl.BlockSpec((N_META, LRU_HALF), lambda b, h, c: (0, h)),
        pl.BlockSpec((tc, LRU_HALF), lambda b, h, c: (b * nch + c, xg_blk + h)),
        pl.BlockSpec((CONV_W, LRU_HALF), lambda b, h, c: (0, h)),
        vec,
        pl.BlockSpec((LRU_BLOCKS // 2, LRU_BW, 2 * LRU_BW), lambda b, h, c: (h, 0, 0)),
        vec, vec, vec, vec,
    ]
    return pl.pallas_call(
        functools.partial(_lru_kernel, tc=tc),
        out_shape=(jax.ShapeDtypeStruct((batch * seq, D_LRU), BF16),
                   jax.ShapeDtypeStruct((batch, 8, D_LRU), F32)),
        grid=(batch, 2, nch),
        in_specs=in_specs,
        out_specs=(pl.BlockSpec((tc, LRU_HALF), lambda b, h, c: (b * nch + c, h)),
                   pl.BlockSpec((1, 8, LRU_HALF), lambda b, h, c: (b, 0, h))),
        scratch_shapes=[pltpu.VMEM((tc, LRU_HALF), F32), pltpu.VMEM((tc, LRU_HALF), F32),
                        pltpu.VMEM((8, LRU_HALF), F32)],
        compiler_params=_cparams(("arbitrary", "arbitrary", "arbitrary")),
        name="conv_rglru",
    )(proj, proj, hist0, proj, cw, cb, wax, ba, bx, lam, h0)


def _merge_kernel(o_ref, y_ref, wa_ref, wl_ref, ga_ref, gl_ref, z_ref):
    ya = jnp.dot(o_ref[...], wa_ref[...], preferred_element_type=F32)
    yl = jnp.dot(y_ref[...], wl_ref[...], preferred_element_type=F32)
    z = jax.nn.sigmoid(ga_ref[...].astype(F32)) * ya + jax.nn.sigmoid(gl_ref[...].astype(F32)) * yl
    z_ref[...] = z.astype(BF16)


def _merge(o_attn, y_lru, wa, wl, proj, tm=512, tn=1024):
    n_tok = o_attn.shape[0]
    ga_blk = GA_COL // tn
    gl_blk = GL_COL // tn
    return pl.pallas_call(
        _merge_kernel,
        out_shape=jax.ShapeDtypeStruct((n_tok, D_MODEL), BF16),
        grid=(D_MODEL // tn, n_tok // tm),
        in_specs=[pl.BlockSpec((tm, D_MODEL), lambda j, i: (i, 0)),
                  pl.BlockSpec((tm, D_MODEL), lambda j, i: (i, 0)),
                  pl.BlockSpec((D_MODEL, tn), lambda j, i: (0, j)),
                  pl.BlockSpec((D_MODEL, tn), lambda j, i: (0, j)),
                  pl.BlockSpec((tm, tn), lambda j, i: (i, ga_blk + j)),
                  pl.BlockSpec((tm, tn), lambda j, i: (i, gl_blk + j))],
        out_specs=pl.BlockSpec((tm, tn), lambda j, i: (i, j)),
        compiler_params=_cparams(("parallel", "parallel")),
        name="gated_merge",
    )(o_attn, y_lru, wa, wl, proj, proj)


def _layernorm(v, g, b):
    mu = jnp.mean(v, axis=-1, keepdims=True)
    dv = v - mu
    var = jnp.mean(dv * dv, axis=-1, keepdims=True)
    return dv * lax.rsqrt(var + LN_EPS) * g + b


def _ln1_kernel(x_ref, z_ref, w_ref, g_ref, b_ref, h_ref, hb_ref):
    mix = jnp.dot(z_ref[...], w_ref[...], preferred_element_type=F32)
    h = _layernorm(DN_ALPHA * x_ref[...] + mix, g_ref[...], b_ref[...])
    h_ref[...] = h
    hb_ref[...] = h.astype(BF16)


def _ln1(x2, z, w_out, g, b, tm=512):
    n_tok = x2.shape[0]
    row = lambda i: (i, 0)
    fixed = lambda i: (0, 0)
    return pl.pallas_call(
        _ln1_kernel,
        out_shape=(jax.ShapeDtypeStruct((n_tok, D_MODEL), F32),
                   jax.ShapeDtypeStruct((n_tok, D_MODEL), BF16)),
        grid=(n_tok // tm,),
        in_specs=[pl.BlockSpec((tm, D_MODEL), row), pl.BlockSpec((tm, D_MODEL), row),
                  pl.BlockSpec((D_MODEL, D_MODEL), fixed),
                  pl.BlockSpec((1, D_MODEL), fixed), pl.BlockSpec((1, D_MODEL), fixed)],
        out_specs=(pl.BlockSpec((tm, D_MODEL), row), pl.BlockSpec((tm, D_MODEL), row)),
        compiler_params=_cparams(("parallel",)),
        name="outproj_ln1",
    )(x2, z, w_out, g, b)


def _router_kernel(h_ref, w_ref, bias_ref, ti_ref, tw_ref, *, tm):
    logits = lax.dot_general(w_ref[...], h_ref[...], (((1,), (1,)), ((), ())), preferred_element_type=F32)
    scores = jax.nn.sigmoid(logits)
    sel = scores + bias_ref[...]
    neg = -jnp.inf
    e_iota = lax.broadcasted_iota(I32, (N_EXPERTS, tm), 0).astype(F32)
    g_iota32 = lax.broadcasted_iota(I32, (GROUP_SIZE, tm), 0).astype(F32)
    gs_rows = []
    for g in range(N_GROUPS):
        blk = sel[g * GROUP_SIZE:(g + 1) * GROUP_SIZE]
        m1 = jnp.max(blk, axis=0, keepdims=True)
        i1 = jnp.min(jnp.where(blk == m1, g_iota32, float(GROUP_SIZE)), axis=0, keepdims=True)
        m2 = jnp.max(jnp.where(g_iota32 == i1, neg, blk), axis=0, keepdims=True)
        gs_rows.append(m1 + m2)
    gs = jnp.concatenate(gs_rows, axis=0)
    g_iota = lax.broadcasted_iota(I32, (N_GROUPS, tm), 0).astype(F32)
    gsel = jnp.zeros((N_GROUPS, tm), F32)
    for _ in range(TOPK_GROUPS):
        m = jnp.max(gs, axis=0, keepdims=True)
        idx = jnp.min(jnp.where(gs == m, g_iota, float(N_GROUPS)), axis=0, keepdims=True)
        hit = g_iota == idx
        gsel = jnp.where(hit, 1.0, gsel)
        gs = jnp.where(hit, neg, gs)
    masked = jnp.concatenate(
        [jnp.where(gsel[g:g + 1] > 0.0, sel[g * GROUP_SIZE:(g + 1) * GROUP_SIZE], neg) for g in range(N_GROUPS)],
        axis=0)
    tis, tws = [], []
    for _ in range(TOP_K):
        m = jnp.max(masked, axis=0, keepdims=True)
        idx = jnp.min(jnp.where(masked == m, e_iota, float(N_EXPERTS)), axis=0, keepdims=True)
        hit = e_iota == idx
        tis.append(idx)
        tws.append(jnp.sum(jnp.where(hit, scores, 0.0), axis=0, keepdims=True))
        masked = jnp.where(hit, neg, masked)
    tw = jnp.concatenate(tws, axis=0)
    ti_ref[...] = jnp.concatenate(tis, axis=0).astype(I32)
    tw_ref[...] = tw / jnp.sum(tw, axis=0, keepdims=True) * ROUTED_SCALE


def _router(hb, wr_t, bias, tm=512):
    n_tok = hb.shape[0]
    return pl.pallas_call(
        functools.partial(_router_kernel, tm=tm),
        out_shape=(jax.ShapeDtypeStruct((TOP_K, n_tok), I32), jax.ShapeDtypeStruct((TOP_K, n_tok), F32)),
        grid=(n_tok // tm,),
        in_specs=[pl.BlockSpec((tm, D_MODEL), lambda i: (i, 0)),
                  pl.BlockSpec((N_EXPERTS, D_MODEL), lambda i: (0, 0)),
                  pl.BlockSpec((N_EXPERTS, 1), lambda i: (0, 0))],
        out_specs=(pl.BlockSpec((TOP_K, tm), lambda i: (0, i)), pl.BlockSpec((TOP_K, tm), lambda i: (0, i))),
        compiler_params=_cparams(("parallel",)),
        name="router_topk",
    )(hb, wr_t, bias)


def _expert_kernel(nblk_ref, bst_ref, xs_ref, wg_ref, wu_ref, wd_ref, ys_ref,
                   xbuf, obuf, isem, osem, wgb, wub, wdb):
    e = pl.program_id(0)
    nb = nblk_ref[e]
    b0 = bst_ref[e]
    nused = bst_ref[N_EXPERTS - 1] + nblk_ref[N_EXPERTS - 1]
    ahead = EXPERT_RING - 1

    def rows(g):
        return pl.ds(pl.multiple_of(g * EXPERT_BLOCK, EXPERT_BLOCK), EXPERT_BLOCK)

    def in_copy(g):
        slot = g & (EXPERT_RING - 1)
        return pltpu.make_async_copy(xs_ref.at[rows(g), :], xbuf.at[slot], isem.at[slot])

    def out_copy(g):
        slot = g & (EXPERT_RING - 1)
        return pltpu.make_async_copy(obuf.at[slot], ys_ref.at[rows(g), :], osem.at[slot])

    @pl.when(nb > 0)
    def _():
        @pl.when(b0 == 0)
        def _():
            for q in range(ahead):
                @pl.when(q < nused)
                def _(q=q):
                    in_copy(q).start(priority=1)

        wgb[...] = wg_ref[0].astype(BF16)
        wub[...] = wu_ref[0].astype(BF16)
        wdb[...] = wd_ref[0].astype(BF16)

        def body(j, carry):
            g = b0 + j
            slot = g & (EXPERT_RING - 1)

            @pl.when(g + ahead < nused)
            def _():
                in_copy(g + ahead).start(priority=1)

            in_copy(g).wait()
            x = xbuf[slot].astype(BF16)
            hid = jax.nn.silu(jnp.dot(x, wgb[...], preferred_element_type=F32)) * jnp.dot(
                x, wub[...], preferred_element_type=F32)
            out = jnp.dot(hid.astype(BF16), wdb[...], preferred_element_type=F32)

            @pl.when(g >= EXPERT_RING)
            def _():
                out_copy(g - EXPERT_RING).wait()

            obuf[slot] = out
            out_copy(g).start(priority=1)
            return carry

        lax.fori_loop(0, nb, body, 0)

    @pl.when(e == N_EXPERTS - 1)
    def _():
        for q in range(EXPERT_RING, 0, -1):
            @pl.when(nused >= q)
            def _(q=q):
                out_copy(nused - q).wait()


def _experts(nblk, bst, xs, wg, wu, wd):
    grid_spec = pltpu.PrefetchScalarGridSpec(
        num_scalar_prefetch=2, grid=(N_EXPERTS,),
        in_specs=[pl.BlockSpec(memory_space=pl.ANY),
                  pl.BlockSpec((1, D_MODEL, D_EXPERT), lambda e, nb, bs: (e, 0, 0)),
                  pl.BlockSpec((1, D_MODEL, D_EXPERT), lambda e, nb, bs: (e, 0, 0)),
                  pl.BlockSpec((1, D_EXPERT, D_MODEL), lambda e, nb, bs: (e, 0, 0))],
        out_specs=pl.BlockSpec(memory_space=pl.ANY),
        scratch_shapes=[pltpu.VMEM((EXPERT_RING, EXPERT_BLOCK, D_MODEL), F32),
                        pltpu.VMEM((EXPERT_RING, EXPERT_BLOCK, D_MODEL), F32),
                        pltpu.SemaphoreType.DMA((EXPERT_RING,)), pltpu.SemaphoreType.DMA((EXPERT_RING,)),
                        pltpu.VMEM((D_MODEL, D_EXPERT), BF16), pltpu.VMEM((D_MODEL, D_EXPERT), BF16),
                        pltpu.VMEM((D_EXPERT, D_MODEL), BF16)])
    return pl.pallas_call(
        _expert_kernel,
        out_shape=jax.ShapeDtypeStruct(xs.shape, F32),
        grid_spec=grid_spec,
        input_output_aliases={2: 0},
        compiler_params=_cparams(("arbitrary",), EXPERT_VMEM_LIMIT),
        name="routed_experts",
    )(nblk, bst, xs, wg, wu, wd)


COMBINE_CHUNK = 16


def _combine_kernel(dc_ref, dn_ref, ys_ref, tw_ref, h_ref, sh_ref, g_ref, b_ref, o_ref, gbuf, sem, *, tm):
    i = pl.program_id(0)
    nsteps = pl.num_programs(0)
    nchunks = tm // COMBINE_CHUNK

    def issue_row(d_ref, s, t):
        for k in range(TOP_K):
            pltpu.make_async_copy(ys_ref.at[pl.ds(d_ref[0, k, t], 1), :],
                                  gbuf.at[s, k, pl.ds(t, 1), :], sem.at[s]).start(priority=k % 2)

    def wait_slabs(s):
        for k in range(TOP_K):
            pltpu.make_async_copy(ys_ref.at[pl.ds(0, tm), :], gbuf.at[s, k], sem.at[s]).wait()

    @pl.when(i == 0)
    def _():
        def body(t, carry):
            issue_row(dc_ref, 0, t)
            return carry
        lax.fori_loop(0, tm, body, 0, unroll=2)

    g = g_ref[...]
    b = b_ref[...]

    for s in (0, 1):
        @pl.when((i & 1) == s)
        def _(s=s):
            wait_slabs(s)
            for c in range(nchunks):
                for tt in range(COMBINE_CHUNK):
                    issue_row(dn_ref, 1 - s, c * COMBINE_CHUNK + tt)
                rs = slice(c * COMBINE_CHUNK, (c + 1) * COMBINE_CHUNK)
                tw = tw_ref[rs, :]
                routed = tw[:, 0:1] * gbuf[s, 0, rs, :]
                for k in range(1, TOP_K):
                    routed = routed + tw[:, k:k + 1] * gbuf[s, k, rs, :]
                v = DN_ALPHA * h_ref[rs, :] + (routed + sh_ref[rs, :].astype(F32))
                o_ref[rs, :] = _layernorm(v, g, b)

            @pl.when(i == nsteps - 1)
            def _():
                wait_slabs(1 - s)


def _combine(dest3, ys, topw, h1, shared, g, b):
    n_tok = h1.shape[0]
    nsteps, _, tm = dest3.shape
    row = lambda i: (i, 0)
    fixed = lambda i: (0, 0)
    return pl.pallas_call(
        functools.partial(_combine_kernel, tm=tm),
        out_shape=jax.ShapeDtypeStruct((n_tok, D_MODEL), F32),
        grid=(nsteps,),
        in_specs=[pl.BlockSpec((1, TOP_K, tm), lambda i: (i, 0, 0), memory_space=pltpu.SMEM),
                  pl.BlockSpec((1, TOP_K, tm), lambda i: (jnp.minimum(i + 1, nsteps - 1), 0, 0),
                               memory_space=pltpu.SMEM),
                  pl.BlockSpec(memory_space=pl.ANY),
                  pl.BlockSpec((tm, TOP_K), row),
                  pl.BlockSpec((tm, D_MODEL), row), pl.BlockSpec((tm, D_MODEL), row),
                  pl.BlockSpec((1, D_MODEL), fixed), pl.BlockSpec((1, D_MODEL), fixed)],
        out_specs=pl.BlockSpec((tm, D_MODEL), row),
        scratch_shapes=[pltpu.VMEM((2, TOP_K, tm, D_MODEL), F32), pltpu.SemaphoreType.DMA((2,))],
        compiler_params=_cparams(("arbitrary",)),
        name="combine_ln2",
    )(dest3, dest3, ys, topw, h1, shared, g, b)


PLAN_TM = 128
LANES = 128
SUBLANES = 8


def _small_int_halves(v):
    hi = jnp.floor(v * (1.0 / 256.0))
    return hi.astype(BF16), (v - hi * 256.0).astype(BF16)


def _plan_kernel(ti_ref, dest_ref, info_ref, rank_s, cntc_s, cntr_s, pst_s, *, tm):
    p = pl.program_id(0)
    i = pl.program_id(1)
    ti = ti_ref[...]
    e_iota = lax.broadcasted_iota(I32, (N_EXPERTS, tm), 0)
    reps = tm // LANES

    @pl.when(jnp.logical_and(p == 0, i == 0))
    def _():
        cntc_s[...] = jnp.zeros(cntc_s.shape, F32)
        cntr_s[...] = jnp.zeros(cntr_s.shape, F32)

    @pl.when(p == 0)
    def _():
        hits = [e_iota == ti[k:k + 1] for k in range(TOP_K)]
        m = jnp.where(hits[0], 1.0, 0.0)
        for k in range(1, TOP_K):
            m = m + jnp.where(hits[k], 1.0, 0.0)
        mb = m.astype(BF16)
        earlier = lax.broadcasted_iota(I32, (tm, tm), 0) < lax.broadcasted_iota(I32, (tm, tm), 1)
        pfx = jnp.dot(mb, jnp.where(earlier, 1.0, 0.0).astype(BF16), preferred_element_type=F32)
        val = pfx + jnp.concatenate([cntc_s[...]] * reps, axis=1)
        rank_s[i] = jnp.concatenate(
            [jnp.sum(jnp.where(hits[k], val, 0.0), axis=0, keepdims=True) for k in range(TOP_K)], axis=0)
        cntc_s[...] += jnp.dot(mb, jnp.ones((tm, LANES), BF16), preferred_element_type=F32)
        cntr_s[...] += lax.dot_general(jnp.ones((8, tm), BF16), mb, (((1,), (1,)), ((), ())),
                                       preferred_element_type=F32)

    @pl.when(jnp.logical_and(p == 1, i == 0))
    def _():
        def n_blocks_of(cnt):
            return jnp.right_shift(cnt.astype(I32) + (EXPERT_BLOCK - 1), EXPERT_BLOCK.bit_length() - 1).astype(F32)

        ee0 = lax.broadcasted_iota(I32, (N_EXPERTS, N_EXPERTS), 0)
        ee1 = lax.broadcasted_iota(I32, (N_EXPERTS, N_EXPERTS), 1)
        hi, lo = _small_int_halves(n_blocks_of(cntc_s[...]))
        lower = jnp.where(ee1 < ee0, 1.0, 0.0).astype(BF16)
        bst_c = 256.0 * jnp.dot(lower, hi, preferred_element_type=F32) + jnp.dot(lower, lo, preferred_element_type=F32)
        pst_s[...] = bst_c * float(EXPERT_BLOCK)
        cnt_r = cntr_s[...]
        nb_r = n_blocks_of(cnt_r)
        hi, lo = _small_int_halves(nb_r)
        upper = jnp.where(ee0 < ee1, 1.0, 0.0).astype(BF16)
        bst_r = 256.0 * jnp.dot(hi, upper, preferred_element_type=F32) + jnp.dot(lo, upper, preferred_element_type=F32)
        sel = lax.broadcasted_iota(I32, (8, N_EXPERTS), 0)
        info = jnp.where(sel == 0, cnt_r, jnp.where(sel == 1, nb_r, jnp.where(sel == 2, bst_r, 0.0)))
        info_ref[...] = info.astype(I32)

    @pl.when(p == 1)
    def _():
        pst = jnp.concatenate([pst_s[...]] * reps, axis=1)
        off = jnp.concatenate(
            [jnp.sum(jnp.where(e_iota == ti[k:k + 1], pst, 0.0), axis=0, keepdims=True) for k in range(TOP_K)], axis=0)
        dest_ref[0] = (rank_s[i] + off).astype(I32)


def _plan(topi_t, tm=PLAN_TM):
    n_tok = topi_t.shape[1]
    nt = n_tok // tm
    return pl.pallas_call(
        functools.partial(_plan_kernel, tm=tm),
        out_shape=(jax.ShapeDtypeStruct((nt, TOP_K, tm), I32), jax.ShapeDtypeStruct((8, N_EXPERTS), I32)),
        grid=(2, nt),
        in_specs=[pl.BlockSpec((TOP_K, tm), lambda p, i: (0, i))],
        out_specs=(pl.BlockSpec((1, TOP_K, tm), lambda p, i: (i * p, 0, 0)),
                   pl.BlockSpec((8, N_EXPERTS), lambda p, i: (0, 0))),
        scratch_shapes=[pltpu.VMEM((nt, TOP_K, tm), F32), pltpu.VMEM((N_EXPERTS, LANES), F32),
                        pltpu.VMEM((8, N_EXPERTS), F32), pltpu.VMEM((N_EXPERTS, LANES), F32)],
        compiler_params=_cparams(("arbitrary", "arbitrary")),
        name="expert_row_plan",
    )(topi_t)


def _dispatch_kernel(cnt_ref, nblk_ref, bst_ref, dest_ref, hp_ref, hb_ref, wg_ref, wu_ref, wd_ref, xs_ref, sh_ref,
                     zrow, sem, zsem, gsem, tsem, *, tm, epg, tpg, n_blocks):
    i = pl.program_id(0)

    @pl.when(i == 0)
    def _():
        zrow[...] = jnp.zeros(zrow.shape, F32)

    nused = bst_ref[N_EXPERTS - 1] + nblk_ref[N_EXPERTS - 1]
    t_lo = jnp.minimum(nused + i * tpg, n_blocks)
    t_hi = jnp.minimum(nused + (i + 1) * tpg, n_blocks)

    def tail_copy(blk):
        return pltpu.make_async_copy(
            zrow, xs_ref.at[pl.ds(pl.multiple_of(blk * EXPERT_BLOCK, EXPERT_BLOCK), EXPERT_BLOCK), :], tsem)

    def tfill(blk, carry):
        tail_copy(blk).start()
        return carry

    def twait(blk, carry):
        tail_copy(blk).wait()
        return carry

    lax.fori_loop(t_lo, t_hi, tfill, 0)

    for t in range(tm):
        for k in range(TOP_K):
            pltpu.make_async_copy(hp_ref.at[pl.ds(t, 1), :], xs_ref.at[pl.ds(dest_ref[0, k, t], 1), :],
                                  sem).start(priority=k % 2)
    hb = hb_ref[...]
    hid = jax.nn.silu(jnp.dot(hb, wg_ref[...], preferred_element_type=F32)) * jnp.dot(
        hb, wu_ref[...], preferred_element_type=F32)
    sh_ref[...] = jnp.dot(hid.astype(BF16), wd_ref[...], preferred_element_type=F32).astype(BF16)

    def zfill(r, carry):
        pltpu.make_async_copy(zrow.at[pl.ds(0, 1), :], xs_ref.at[pl.ds(r, 1), :], zsem).start()
        return carry

    def zwait(r, carry):
        pltpu.make_async_copy(zrow.at[pl.ds(0, 1), :], xs_ref.at[pl.ds(0, 1), :], zsem).wait()
        return carry

    def group_copy(grp):
        return pltpu.make_async_copy(
            zrow.at[pl.ds(0, SUBLANES), :], xs_ref.at[pl.ds(pl.multiple_of(grp * SUBLANES, SUBLANES), SUBLANES), :],
            gsem)

    def gfill(grp, carry):
        group_copy(grp).start()
        return carry

    def gwait(grp, carry):
        group_copy(grp).wait()
        return carry

    bounds = []
    for q in range(epg):
        e = i * epg + q
        lo = bst_ref[e] * EXPERT_BLOCK + cnt_ref[e]
        hi = (bst_ref[e] + nblk_ref[e]) * EXPERT_BLOCK
        mid = jnp.minimum(jnp.bitwise_and(lo + (SUBLANES - 1), -SUBLANES), hi)
        g_lo = jnp.right_shift(mid, SUBLANES.bit_length() - 1)
        g_hi = jnp.right_shift(hi, SUBLANES.bit_length() - 1)
        bounds.append((lo, mid, g_lo, g_hi))
        lax.fori_loop(lo, mid, zfill, 0)
        lax.fori_loop(g_lo, g_hi, gfill, 0)

    for k in range(TOP_K):
        pltpu.make_async_copy(hp_ref, xs_ref.at[pl.ds(0, tm), :], sem).wait()
    for lo, mid, g_lo, g_hi in bounds:
        lax.fori_loop(lo, mid, zwait, 0)
        lax.fori_loop(g_lo, g_hi, gwait, 0)
    lax.fori_loop(t_lo, t_hi, twait, 0)


def _dispatch(cnt, nblk, bst, dest3, hp, hb, wgs, wus, wds, n_rows):
    nsteps, _, tm = dest3.shape
    n_tok = hp.shape[0]
    n_blocks = n_rows // EXPERT_BLOCK
    assert N_EXPERTS % nsteps == 0
    row = lambda i, *_: (i, 0)
    fixed = lambda i, *_: (0, 0)
    grid_spec = pltpu.PrefetchScalarGridSpec(
        num_scalar_prefetch=3, grid=(nsteps,),
        in_specs=[pl.BlockSpec((1, TOP_K, tm), lambda i, *_: (i, 0, 0), memory_space=pltpu.SMEM),
                  pl.BlockSpec((tm, D_MODEL), row), pl.BlockSpec((tm, D_MODEL), row),
                  pl.BlockSpec((D_MODEL, D_EXPERT), fixed), pl.BlockSpec((D_MODEL, D_EXPERT), fixed),
                  pl.BlockSpec((D_EXPERT, D_MODEL), fixed)],
        out_specs=(pl.BlockSpec(memory_space=pl.ANY), pl.BlockSpec((tm, D_MODEL), row)),
        scratch_shapes=[pltpu.VMEM((EXPERT_BLOCK, D_MODEL), F32), pltpu.SemaphoreType.DMA(()),
                        pltpu.SemaphoreType.DMA(()), pltpu.SemaphoreType.DMA(()), pltpu.SemaphoreType.DMA(())])
    return pl.pallas_call(
        functools.partial(_dispatch_kernel, tm=tm, epg=N_EXPERTS // nsteps, tpg=-(-n_blocks // nsteps),
                          n_blocks=n_blocks),
        out_shape=(jax.ShapeDtypeStruct((n_rows, D_MODEL), F32), jax.ShapeDtypeStruct((n_tok, D_MODEL), BF16)),
        grid_spec=grid_spec,
        compiler_params=_cparams(("arbitrary",)),
        name="dispatch_shared",
    )(cnt, nblk, bst, dest3, hp, hb, wgs, wus, wds)


def _rope_tables(n_pos):
    half = ROT_DIM // 2
    inv_freq = ROPE_THETA ** (-jnp.arange(half, dtype=F32) / half)
    ang = jnp.arange(n_pos).astype(F32)[:, None] * inv_freq[None, :]
    cos, sin = jnp.cos(ang), jnp.sin(ang)
    c = jnp.concatenate([cos, cos, jnp.ones((n_pos, HEAD_DIM - ROT_DIM), F32)], axis=1)
    s = jnp.concatenate([sin, sin, jnp.zeros((n_pos, HEAD_DIM - ROT_DIM), F32)], axis=1)
    return c, s


def kernel(x, meta_tokens, w_in, conv_w, conv_b, w_rg_a, b_rg_a, w_rg_x, b_rg_x, rg_lambda, attn_sinks, w_o_attn, w_o_lru, w_out, ln1_g, ln1_b, w_router, router_bias, w_gate_e, w_up_e, w_down_e, w_gate_s, w_up_s, w_down_s, ln2_g, ln2_b):
    batch, seq, _ = x.shape
    n_tok = batch * seq
    assert w_in.shape[0] == 1 and seq % 512 == 0 and meta_tokens.shape[0] == N_META
    x2 = x.reshape(n_tok, D_MODEL)
    row = lambda v: v.reshape(1, -1).astype(F32)

    wax = jnp.concatenate([w_rg_a[0], w_rg_x[0]], axis=-1).astype(BF16)
    c_t, s_t = _rope_tables(N_META + seq)
    tabs_meta = (c_t[:N_META], s_t[:N_META])
    tabs_real = (c_t[N_META:], s_t[N_META:])

    projm = _matmul(meta_tokens, w_in[0], BF16, N_META, 1024)
    lru_args = (conv_w[0], row(conv_b[0]), wax, row(b_rg_a[0]), row(b_rg_x[0]), row(rg_lambda[0]))
    _, h_meta = _lru(projm, jnp.zeros((N_META, D_LRU), BF16), jnp.zeros((1, D_LRU), F32), *lru_args,
                     batch=1, seq=N_META, tc=N_META)

    proj = _matmul(x2, w_in[0], BF16, 1024, 1024)
    o_attn = _attention(proj, projm, attn_sinks[0].astype(F32), tabs_real, tabs_meta, batch, seq)
    y_lru, _ = _lru(proj, projm[:, XR_COL:XR_COL + D_LRU], h_meta[0, 7:8], *lru_args, batch=batch, seq=seq, tc=LRU_SUB)
    z = _merge(o_attn, y_lru, w_o_attn[0].astype(BF16), w_o_lru[0].astype(BF16), proj)
    h1, h1b = _ln1(x2, z, w_out[0].astype(BF16), row(ln1_g[0]), row(ln1_b[0]))

    topi_t, topw_t = _router(h1b, w_router[0].T.astype(BF16), router_bias[0].reshape(N_EXPERTS, 1).astype(F32))
    dest3, info = _plan(topi_t)
    cnt, nblk, bst = info[0], info[1], info[2]
    n_rows = -(-(n_tok * TOP_K + N_EXPERTS * (EXPERT_BLOCK - 1)) // EXPERT_BLOCK) * EXPERT_BLOCK
    xs, shared = _dispatch(cnt, nblk, bst, dest3, h1, h1b,w_gate_s[0].astype(BF16), w_up_s[0].astype(BF16),
                           w_down_s[0].astype(BF16), n_rows)
    ys = _experts(nblk, bst, xs, w_gate_e[0], w_up_e[0], w_down_e[0])
    out = _combine(dest3, ys, topw_t.T, h1, shared, row(ln2_g[0]), row(ln2_b[0]))
    return out.reshape(batch, seq, D_MODEL)
```

```python
import functools

import jax
import jax.numpy as jnp
from jax import lax
from jax.experimental import pallas as pl
from jax.experimental.pallas import tpu as pltpu

F32 = jnp.float32
BF16 = jnp.bfloat16
U32 = jnp.uint32
I32 = jnp.int32

D_MODEL = 2048
N_META = 16
N_Q_HEADS = 16
N_KV_HEADS = 4
HEAD_DIM = 128
Q_GROUP = N_Q_HEADS // N_KV_HEADS
ROT_DIM = HEAD_DIM // 4
ROPE_THETA = 500000.0
ATTN_BLOCK = 128
D_LRU = D_MODEL
LRU_BLOCKS = 16
LRU_BW = D_LRU // LRU_BLOCKS
CONV_W = 4
RGLRU_C = 8.0
N_EXPERTS = 256
TOP_K = 8
N_GROUPS = 8
GROUP_SIZE = N_EXPERTS // N_GROUPS
TOPK_GROUPS = 4
D_EXPERT = 512
ROUTED_SCALE = 2.5
EXPERT_BLOCK = 128
EXPERT_RING = 8
DN_ALPHA = 2.0 ** 0.25
LN_EPS = 1e-5
NEG_INF = -1e30
KV_DIM = N_KV_HEADS * HEAD_DIM
Q_DIM = N_Q_HEADS * HEAD_DIM
K_COLBLK = Q_DIM // KV_DIM
V_COLBLK = K_COLBLK + 1
XR_COL = Q_DIM + 2 * KV_DIM
XG_COL = XR_COL + D_LRU
GA_COL = XG_COL + D_LRU
GL_COL = GA_COL + D_MODEL
LRU_HALF = D_LRU // 2
LRU_SUB = 256
VMEM_LIMIT = 50 * 1024 * 1024
EXPERT_VMEM_LIMIT = (2 * 3 * D_MODEL * D_EXPERT * 4 + 3 * D_MODEL * D_EXPERT * 2
                     + 2 * EXPERT_RING * EXPERT_BLOCK * D_MODEL * 4 + 8 * 1024 * 1024)


def _cparams(sem, vmem_limit=VMEM_LIMIT):
    return pltpu.CompilerParams(dimension_semantics=sem, vmem_limit_bytes=vmem_limit)


def _mm_kernel(a_ref, b_ref, o_ref, b_bf):
    @pl.when(pl.program_id(1) == 0)
    def _():
        b_bf[...] = b_ref[...].astype(BF16)

    o_ref[...] = jnp.dot(a_ref[...].astype(BF16), b_bf[...], preferred_element_type=F32).astype(o_ref.dtype)


def _matmul(a, b, out_dtype, tm, tn):
    m, k = a.shape
    n = b.shape[1]
    return pl.pallas_call(
        _mm_kernel,
        out_shape=jax.ShapeDtypeStruct((m, n), out_dtype),
        grid=(n // tn, m // tm),
        in_specs=[pl.BlockSpec((tm, k), lambda j, i: (i, 0)),
                  pl.BlockSpec((k, tn), lambda j, i: (0, j))],
        out_specs=pl.BlockSpec((tm, tn), lambda j, i: (i, j)),
        scratch_shapes=[pltpu.VMEM((k, tn), BF16)],
        compiler_params=_cparams(("arbitrary", "arbitrary")),
        name="dense_matmul",
    )(a, b)


def _rope(x, rot, c, s):
    return x.astype(F32) * c + jnp.dot(x, rot, preferred_element_type=F32) * s


def _attn_kernel(sink_ref, q_ref, kc_ref, kp_ref, vc_ref, vp_ref, km_ref, vm_ref,
                 cq_ref, sq_ref, cp_ref, sp_ref, cm_ref, sm_ref, o_ref, *, tq):
    n = pl.program_id(1)
    nsub = tq // ATTN_BLOCK
    scale = HEAD_DIM ** -0.5
    cq, sq = cq_ref[...], sq_ref[...]
    half = ROT_DIM // 2
    rj = lax.broadcasted_iota(I32, (HEAD_DIM, HEAD_DIM), 0)
    ri = lax.broadcasted_iota(I32, (HEAD_DIM, HEAD_DIM), 1)
    rot = (jnp.where(jnp.logical_and(ri < half, rj == ri + half), -1.0, 0.0)
           + jnp.where(jnp.logical_and(jnp.logical_and(ri >= half, ri < ROT_DIM), rj == ri - half), 1.0, 0.0)
           ).astype(BF16)
    rows = Q_GROUP * ATTN_BLOCK
    r_idx = lax.broadcasted_iota(I32, (rows, ATTN_BLOCK), 0) % ATTN_BLOCK
    c_idx = lax.broadcasted_iota(I32, (rows, ATTN_BLOCK), 1)
    cur_ok = c_idx <= r_idx
    prev_ok = c_idx > r_idx
    first_prev_ok = (c_idx - r_idx) > jnp.where(n > 0, 0, ATTN_BLOCK)
    meta_ok = c_idx < N_META
    pad_rows = jnp.zeros((ATTN_BLOCK - N_META, HEAD_DIM), BF16)
    ones_blk = jnp.ones((3 * ATTN_BLOCK, HEAD_DIM), BF16)
    dn = (((1,), (1,)), ((), ()))
    for g in range(N_KV_HEADS):
        hs = slice(g * HEAD_DIM, (g + 1) * HEAD_DIM)
        k_g = _rope(kc_ref[:, hs], rot, cq, sq).astype(BF16)
        kp_g = _rope(kp_ref[:, hs], rot, cp_ref[...], sp_ref[...]).astype(BF16)
        km_g = jnp.concatenate([_rope(km_ref[:, hs], rot, cm_ref[...], sm_ref[...]).astype(BF16), pad_rows], axis=0)
        v_g = vc_ref[:, hs]
        vp_g = vp_ref[:, hs]
        vm_g = jnp.concatenate([vm_ref[:, hs], pad_rows], axis=0)
        sink = jnp.concatenate(
            [jnp.full((ATTN_BLOCK, 1), sink_ref[g * Q_GROUP + j], F32) for j in range(Q_GROUP)], axis=0)
        for sub in range(nsub):
            rs = slice(sub * ATTN_BLOCK, (sub + 1) * ATTN_BLOCK)
            q_st = jnp.concatenate(
                [_rope(q_ref[rs, (g * Q_GROUP + j) * HEAD_DIM:(g * Q_GROUP + j + 1) * HEAD_DIM], rot,
                       cq[rs], sq[rs]) for j in range(Q_GROUP)], axis=0).astype(BF16)
            if sub == 0:
                k_all = jnp.concatenate([kp_g, k_g[rs], km_g], axis=0)
                v_all = jnp.concatenate([vp_g, v_g[rs], vm_g], axis=0)
                p_ok = first_prev_ok
            else:
                both = slice((sub - 1) * ATTN_BLOCK, (sub + 1) * ATTN_BLOCK)
                k_all = jnp.concatenate([k_g[both], km_g], axis=0)
                v_all = jnp.concatenate([v_g[both], vm_g], axis=0)
                p_ok = prev_ok
            sc = lax.dot_general(q_st, k_all, dn, preferred_element_type=F32) * scale
            s_p = jnp.where(p_ok, sc[:, :ATTN_BLOCK], NEG_INF)
            s_c = jnp.where(cur_ok, sc[:, ATTN_BLOCK:2 * ATTN_BLOCK], NEG_INF)
            s_m = jnp.where(meta_ok, sc[:, 2 * ATTN_BLOCK:], NEG_INF)
            m = jnp.maximum(jnp.max(jnp.maximum(jnp.maximum(s_p, s_c), s_m), axis=1, keepdims=True), sink)
            p = jnp.concatenate([jnp.exp(s_p - m), jnp.exp(s_c - m), jnp.exp(s_m - m)], axis=1).astype(BF16)
            od = jnp.dot(p, jnp.concatenate([v_all, ones_blk], axis=1), preferred_element_type=F32)
            o = od[:, :HEAD_DIM] / (od[:, HEAD_DIM:] + jnp.exp(sink - m))
            for j in range(Q_GROUP):
                h = g * Q_GROUP + j
                o_ref[rs, h * HEAD_DIM:(h + 1) * HEAD_DIM] = o[j * ATTN_BLOCK:(j + 1) * ATTN_BLOCK].astype(BF16)


def _attention(proj, projm, sinks, tabs_real, tabs_meta, batch, seq, tq=512):
    n_tok = batch * seq
    nq = seq // tq
    sub_per = tq // ATTN_BLOCK

    def prev_blk(b, n):
        return jnp.maximum(b * (seq // ATTN_BLOCK) + n * sub_per - 1, 0)

    tab_spec = pl.BlockSpec((tq, HEAD_DIM), lambda b, n, s: (n, 0))
    tabp_spec = pl.BlockSpec((ATTN_BLOCK, HEAD_DIM), lambda b, n, s: (jnp.maximum(n * sub_per - 1, 0), 0))
    tabm_spec = pl.BlockSpec((N_META, HEAD_DIM), lambda b, n, s: (0, 0))
    in_specs = [
        pl.BlockSpec((tq, D_MODEL), lambda b, n, s: (b * nq + n, 0)),
        pl.BlockSpec((tq, KV_DIM), lambda b, n, s: (b * nq + n, K_COLBLK)),
        pl.BlockSpec((ATTN_BLOCK, KV_DIM), lambda b, n, s: (prev_blk(b, n), K_COLBLK)),
        pl.BlockSpec((tq, KV_DIM), lambda b, n, s: (b * nq + n, V_COLBLK)),
        pl.BlockSpec((ATTN_BLOCK, KV_DIM), lambda b, n, s: (prev_blk(b, n), V_COLBLK)),
        pl.BlockSpec((N_META, KV_DIM), lambda b, n, s: (0, K_COLBLK)),
        pl.BlockSpec((N_META, KV_DIM), lambda b, n, s: (0, V_COLBLK)),
        tab_spec, tab_spec, tabp_spec, tabp_spec, tabm_spec, tabm_spec,
    ]
    grid_spec = pltpu.PrefetchScalarGridSpec(
        num_scalar_prefetch=1, grid=(batch, nq), in_specs=in_specs,
        out_specs=pl.BlockSpec((tq, D_MODEL), lambda b, n, s: (b * nq + n, 0)))
    return pl.pallas_call(
        functools.partial(_attn_kernel, tq=tq),
        out_shape=jax.ShapeDtypeStruct((n_tok, D_MODEL), BF16),
        grid_spec=grid_spec,
        compiler_params=_cparams(("parallel", "parallel")),
        name="swa_attention",
    )(sinks, proj, proj, proj, proj, proj, projm, projm,
      tabs_real[0], tabs_real[1], tabs_real[0], tabs_real[1], tabs_meta[0], tabs_meta[1])


def _softplus(z):
    return jnp.maximum(z, 0.0) + jnp.log1p(jnp.exp(-jnp.abs(z)))


def _sigmoid(z):
    return 0.5 * jnp.tanh(0.5 * z) + 0.5


def _lru_kernel(xr_ref, prev_ref, hist0_ref, xg_ref, cw_ref, cb_ref, wax_ref, ba_ref, bx_ref, lam_ref, h0_ref,
                y_ref, hl_ref, a_s, u_s, hcar, *, tc):
    c = pl.program_id(2)
    width = a_s.shape[1]
    blocks = width // LRU_BW

    @pl.when(c == 0)
    def _():
        hcar[...] = jnp.broadcast_to(h0_ref[...], hcar.shape)

    first = c == 0
    sub = min(tc, LRU_SUB)

    def gate_rows(sc, carry):
        r0 = pl.multiple_of(sc * sub, sub)
        rs = pl.ds(r0, sub)
        before = pl.ds(pl.multiple_of(jnp.maximum(r0 - N_META, 0), N_META), N_META)
        for nb in range(blocks):
            cs = slice(nb * LRU_BW, (nb + 1) * LRU_BW)
            hist = jnp.where(sc > 0, xr_ref[before, cs], jnp.where(first, hist0_ref[:, cs], prev_ref[:, cs]))
            x = xr_ref[rs, cs].astype(F32)
            ext = jnp.concatenate([hist.astype(F32), x], axis=0)
            cw = cw_ref[:, cs]
            y = cb_ref[:, cs] + x * cw[CONV_W - 1:CONV_W]
            for d in range(1, CONV_W):
                y = y + pltpu.roll(ext, d, 0)[N_META:] * cw[CONV_W - 1 - d:CONV_W - d]
            gates = jnp.dot(y.astype(BF16), wax_ref[nb], preferred_element_type=F32)
            r = _sigmoid(gates[:, :LRU_BW] + ba_ref[:, cs])
            ig = _sigmoid(gates[:, LRU_BW:] + bx_ref[:, cs])
            log_a = r * ((-RGLRU_C) * _softplus(-lam_ref[:, cs]))
            a = jnp.exp(log_a)
            a_s[rs, cs] = a
            u_s[rs, cs] = jnp.sqrt(-jnp.tanh(log_a) * (a * a + 1.0)) * (ig * y)
        return carry

    lax.fori_loop(0, tc // sub, gate_rows, 0)
    row = lax.broadcasted_iota(I32, (8, width), 0)

    def body(i, h):
        sl = pl.ds(pl.multiple_of(i * 8, 8), 8)
        a = a_s[sl, :]
        u = u_s[sl, :]
        for d in (1, 2, 4):
            ok = row >= d
            u = jnp.where(ok, a * pltpu.roll(u, d, 0) + u, u)
            a = jnp.where(ok, a * pltpu.roll(a, d, 0), a)
        hs = a * h + u
        u_s[sl, :] = hs
        return jnp.broadcast_to(hs[7:8, :], (8, width))

    h_fin = lax.fori_loop(0, tc // 8, body, hcar[...], unroll=2)
    hcar[...] = h_fin
    hl_ref[0] = h_fin

    def gate_out(sc, carry):
        rs = pl.ds(pl.multiple_of(sc * sub, sub), sub)
        for nb in range(blocks):
            cs = slice(nb * LRU_BW, (nb + 1) * LRU_BW)
            y_ref[rs, cs] = (u_s[rs, cs] * jax.nn.gelu(xg_ref[rs, cs].astype(F32))).astype(BF16)
        return carry

    lax.fori_loop(0, tc // sub, gate_out, 0)


def _lru(proj, hist0, h0, cw, cb, wax, ba, bx, lam, batch, seq, tc):
    nch = seq // tc
    per16 = tc // N_META

    xr_blk = XR_COL // LRU_HALF
    xg_blk = XG_COL // LRU_HALF

    def prev_idx(b, h, c):
        return (jnp.maximum(b * (seq // N_META) + c * per16 - 1, 0), xr_blk + h)

    vec = pl.BlockSpec((1, LRU_HALF), lambda b, h, c: (0, h))
    in_specs = [
        pl.BlockSpec((tc, LRU_HALF), lambda b, h, c: (b * nch + c, xr_blk + h)),
        pl.BlockSpec((N_META, LRU_HALF), prev_idx),
        pl.BlockSpec((N_META, LRU_HALF), lambda b, h, c: (0, h)),
        pl.BlockSpec((tc, LRU_HALF), lambda b, h, c: (b * nch + c, xg_blk + h)),
        pl.BlockSpec((CONV_W, LRU_HALF), lambda b, h, c: (0, h)),
        vec,
        pl.BlockSpec((LRU_BLOCKS // 2, LRU_BW, 2 * LRU_BW), lambda b, h, c: (h, 0, 0)),
        vec, vec, vec, vec,
    ]
    return pl.pallas_call(
        functools.partial(_lru_kernel, tc=tc),
        out_shape=(jax.ShapeDtypeStruct((batch * seq, D_LRU), BF16),
                   jax.ShapeDtypeStruct((batch, 8, D_LRU), F32)),
        grid=(batch, 2, nch),
        in_specs=in_specs,
        out_specs=(pl.BlockSpec((tc, LRU_HALF), lambda b, h, c: (b * nch + c, h)),
                   pl.BlockSpec((1, 8, LRU_HALF), lambda b, h, c: (b, 0, h))),
        scratch_shapes=[pltpu.VMEM((tc, LRU_HALF), F32), pltpu.VMEM((tc, LRU_HALF), F32),
                        pltpu.VMEM((8, LRU_HALF), F32)],
        compiler_params=_cparams(("arbitrary", "arbitrary", "arbitrary")),
        name="conv_rglru",
    )(proj, proj, hist0, proj, cw, cb, wax, ba, bx, lam, h0)


def _merge_kernel(o_ref, y_ref, wa_ref, wl_ref, ga_ref, gl_ref, z_ref):
    ya = jnp.dot(o_ref[...], wa_ref[...], preferred_element_type=F32)
    yl = jnp.dot(y_ref[...], wl_ref[...], preferred_element_type=F32)
    z = jax.nn.sigmoid(ga_ref[...].astype(F32)) * ya + jax.nn.sigmoid(gl_ref[...].astype(F32)) * yl
    z_ref[...] = z.astype(BF16)


def _merge(o_attn, y_lru, wa, wl, proj, tm=512, tn=1024):
    n_tok = o_attn.shape[0]
    ga_blk = GA_COL // tn
    gl_blk = GL_COL // tn
    return pl.pallas_call(
        _merge_kernel,
        out_shape=jax.ShapeDtypeStruct((n_tok, D_MODEL), BF16),
        grid=(D_MODEL // tn, n_tok // tm),
        in_specs=[pl.BlockSpec((tm, D_MODEL), lambda j, i: (i, 0)),
                  pl.BlockSpec((tm, D_MODEL), lambda j, i: (i, 0)),
                  pl.BlockSpec((D_MODEL, tn), lambda j, i: (0, j)),
                  pl.BlockSpec((D_MODEL, tn), lambda j, i: (0, j)),
                  pl.BlockSpec((tm, tn), lambda j, i: (i, ga_blk + j)),
                  pl.BlockSpec((tm, tn), lambda j, i: (i, gl_blk + j))],
        out_specs=pl.BlockSpec((tm, tn), lambda j, i: (i, j)),
        compiler_params=_cparams(("parallel", "parallel")),
        name="gated_merge",
    )(o_attn, y_lru, wa, wl, proj, proj)


def _layernorm(v, g, b):
    mu = jnp.mean(v, axis=-1, keepdims=True)
    dv = v - mu
    var = jnp.mean(dv * dv, axis=-1, keepdims=True)
    return dv * lax.rsqrt(var + LN_EPS) * g + b


def _ln1_kernel(x_ref, z_ref, w_ref, g_ref, b_ref, h_ref, hb_ref):
    mix = jnp.dot(z_ref[...], w_ref[...], preferred_element_type=F32)
    h = _layernorm(DN_ALPHA * x_ref[...] + mix, g_ref[...], b_ref[...])
    h_ref[...] = h
    hb_ref[...] = h.astype(BF16)


def _ln1(x2, z, w_out, g, b, tm=512):
    n_tok = x2.shape[0]
    row = lambda i: (i, 0)
    fixed = lambda i: (0, 0)
    return pl.pallas_call(
        _ln1_kernel,
        out_shape=(jax.ShapeDtypeStruct((n_tok, D_MODEL), F32),
                   jax.ShapeDtypeStruct((n_tok, D_MODEL), BF16)),
        grid=(n_tok // tm,),
        in_specs=[pl.BlockSpec((tm, D_MODEL), row), pl.BlockSpec((tm, D_MODEL), row),
                  pl.BlockSpec((D_MODEL, D_MODEL), fixed),
                  pl.BlockSpec((1, D_MODEL), fixed), pl.BlockSpec((1, D_MODEL), fixed)],
        out_specs=(pl.BlockSpec((tm, D_MODEL), row), pl.BlockSpec((tm, D_MODEL), row)),
        compiler_params=_cparams(("parallel",)),
        name="outproj_ln1",
    )(x2, z, w_out, g, b)


def _router_kernel(h_ref, w_ref, bias_ref, ti_ref, tw_ref, *, tm):
    logits = lax.dot_general(w_ref[...], h_ref[...], (((1,), (1,)), ((), ())), preferred_element_type=F32)
    scores = jax.nn.sigmoid(logits)
    sel = scores + bias_ref[...]
    neg = -jnp.inf
    e_iota = lax.broadcasted_iota(I32, (N_EXPERTS, tm), 0).astype(F32)
    g_iota32 = lax.broadcasted_iota(I32, (GROUP_SIZE, tm), 0).astype(F32)
    gs_rows = []
    for g in range(N_GROUPS):
        blk = sel[g * GROUP_SIZE:(g + 1) * GROUP_SIZE]
        m1 = jnp.max(blk, axis=0, keepdims=True)
        i1 = jnp.min(jnp.where(blk == m1, g_iota32, float(GROUP_SIZE)), axis=0, keepdims=True)
        m2 = jnp.max(jnp.where(g_iota32 == i1, neg, blk), axis=0, keepdims=True)
        gs_rows.append(m1 + m2)
    gs = jnp.concatenate(gs_rows, axis=0)
    g_iota = lax.broadcasted_iota(I32, (N_GROUPS, tm), 0).astype(F32)
    gsel = jnp.zeros((N_GROUPS, tm), F32)
    for _ in range(TOPK_GROUPS):
        m = jnp.max(gs, axis=0, keepdims=True)
        idx = jnp.min(jnp.where(gs == m, g_iota, float(N_GROUPS)), axis=0, keepdims=True)
        hit = g_iota == idx
        gsel = jnp.where(hit, 1.0, gsel)
        gs = jnp.where(hit, neg, gs)
    masked = jnp.concatenate(
        [jnp.where(gsel[g:g + 1] > 0.0, sel[g * GROUP_SIZE:(g + 1) * GROUP_SIZE], neg) for g in range(N_GROUPS)],
        axis=0)
    tis, tws = [], []
    for _ in range(TOP_K):
        m = jnp.max(masked, axis=0, keepdims=True)
        idx = jnp.min(jnp.where(masked == m, e_iota, float(N_EXPERTS)), axis=0, keepdims=True)
        hit = e_iota == idx
        tis.append(idx)
        tws.append(jnp.sum(jnp.where(hit, scores, 0.0), axis=0, keepdims=True))
        masked = jnp.where(hit, neg, masked)
    tw = jnp.concatenate(tws, axis=0)
    ti_ref[...] = jnp.concatenate(tis, axis=0).astype(I32)
    tw_ref[...] = tw / jnp.sum(tw, axis=0, keepdims=True) * ROUTED_SCALE


def _router(hb, wr_t, bias, tm=512):
    n_tok = hb.shape[0]
    return pl.pallas_call(
        functools.partial(_router_kernel, tm=tm),
        out_shape=(jax.ShapeDtypeStruct((TOP_K, n_tok), I32), jax.ShapeDtypeStruct((TOP_K, n_tok), F32)),
        grid=(n_tok // tm,),
        in_specs=[pl.BlockSpec((tm, D_MODEL), lambda i: (i, 0)),
                  pl.BlockSpec((N_EXPERTS, D_MODEL), lambda i: (0, 0)),
                  pl.BlockSpec((N_EXPERTS, 1), lambda i: (0, 0))],
        out_specs=(pl.BlockSpec((TOP_K, tm), lambda i: (0, i)), pl.BlockSpec((TOP_K, tm), lambda i: (0, i))),
        compiler_params=_cparams(("parallel",)),
        name="router_topk",
    )(hb, wr_t, bias)


def _expert_kernel(nblk_ref, bst_ref, xs_ref, wg_ref, wu_ref, wd_ref, ys_ref,
                   xbuf, obuf, isem, osem, wgb, wub, wdb):
    e = pl.program_id(0)
    nb = nblk_ref[e]
    b0 = bst_ref[e]
    nused = bst_ref[N_EXPERTS - 1] + nblk_ref[N_EXPERTS - 1]
    ahead = EXPERT_RING - 1

    def rows(g):
        return pl.ds(pl.multiple_of(g * EXPERT_BLOCK, EXPERT_BLOCK), EXPERT_BLOCK)

    def in_copy(g):
        slot = g & (EXPERT_RING - 1)
        return pltpu.make_async_copy(xs_ref.at[rows(g), :], xbuf.at[slot], isem.at[slot])

    def out_copy(g):
        slot = g & (EXPERT_RING - 1)
        return pltpu.make_async_copy(obuf.at[slot], ys_ref.at[rows(g), :], osem.at[slot])

    @pl.when(nb > 0)
    def _():
        @pl.when(b0 == 0)
        def _():
            for q in range(ahead):
                @pl.when(q < nused)
                def _(q=q):
                    in_copy(q).start(priority=1)

        wgb[...] = wg_ref[0].astype(BF16)
        wub[...] = wu_ref[0].astype(BF16)
        wdb[...] = wd_ref[0].astype(BF16)

        def body(j, carry):
            g = b0 + j
            slot = g & (EXPERT_RING - 1)

            @pl.when(g + ahead < nused)
            def _():
                in_copy(g + ahead).start(priority=1)

            in_copy(g).wait()
            x = xbuf[slot].astype(BF16)
            hid = jax.nn.silu(jnp.dot(x, wgb[...], preferred_element_type=F32)) * jnp.dot(
                x, wub[...], preferred_element_type=F32)
            out = jnp.dot(hid.astype(BF16), wdb[...], preferred_element_type=F32)

            @pl.when(g >= EXPERT_RING)
            def _():
                out_copy(g - EXPERT_RING).wait()

            obuf[slot] = out
            out_copy(g).start(priority=1)
            return carry

        lax.fori_loop(0, nb, body, 0)

    @pl.when(e == N_EXPERTS - 1)
    def _():
        for q in range(EXPERT_RING, 0, -1):
            @pl.when(nused >= q)
            def _(q=q):
                out_copy(nused - q).wait()


def _experts(nblk, bst, xs, wg, wu, wd):
    grid_spec = pltpu.PrefetchScalarGridSpec(
        num_scalar_prefetch=2, grid=(N_EXPERTS,),
        in_specs=[pl.BlockSpec(memory_space=pl.ANY),
                  pl.BlockSpec((1, D_MODEL, D_EXPERT), lambda e, nb, bs: (e, 0, 0)),
                  pl.BlockSpec((1, D_MODEL, D_EXPERT), lambda e, nb, bs: (e, 0, 0)),
                  pl.BlockSpec((1, D_EXPERT, D_MODEL), lambda e, nb, bs: (e, 0, 0))],
        out_specs=pl.BlockSpec(memory_space=pl.ANY),
        scratch_shapes=[pltpu.VMEM((EXPERT_RING, EXPERT_BLOCK, D_MODEL), F32),
                        pltpu.VMEM((EXPERT_RING, EXPERT_BLOCK, D_MODEL), F32),
                        pltpu.SemaphoreType.DMA((EXPERT_RING,)), pltpu.SemaphoreType.DMA((EXPERT_RING,)),
                        pltpu.VMEM((D_MODEL, D_EXPERT), BF16), pltpu.VMEM((D_MODEL, D_EXPERT), BF16),
                        pltpu.VMEM((D_EXPERT, D_MODEL), BF16)])
    return pl.pallas_call(
        _expert_kernel,
        out_shape=jax.ShapeDtypeStruct(xs.shape, F32),
        grid_spec=grid_spec,
        input_output_aliases={2: 0},
        compiler_params=_cparams(("arbitrary",), EXPERT_VMEM_LIMIT),
        name="routed_experts",
    )(nblk, bst, xs, wg, wu, wd)


COMBINE_CHUNK = 16


def _combine_kernel(dc_ref, dn_ref, ys_ref, tw_ref, h_ref, sh_ref, g_ref, b_ref, o_ref, gbuf, sem, *, tm):
    i = pl.program_id(0)
    nsteps = pl.num_programs(0)
    nchunks = tm // COMBINE_CHUNK

    def issue_row(d_ref, s, t):
        for k in range(TOP_K):
            pltpu.make_async_copy(ys_ref.at[pl.ds(d_ref[0, k, t], 1), :],
                                  gbuf.at[s, k, pl.ds(t, 1), :], sem.at[s]).start(priority=k % 2)

    def wait_slabs(s):
        for k in range(TOP_K):
            pltpu.make_async_copy(ys_ref.at[pl.ds(0, tm), :], gbuf.at[s, k], sem.at[s]).wait()

    @pl.when(i == 0)
    def _():
        def body(t, carry):
            issue_row(dc_ref, 0, t)
            return carry
        lax.fori_loop(0, tm, body, 0, unroll=2)

    g = g_ref[...]
    b = b_ref[...]

    for s in (0, 1):
        @pl.when((i & 1) == s)
        def _(s=s):
            wait_slabs(s)
            for c in range(nchunks):
                for tt in range(COMBINE_CHUNK):
                    issue_row(dn_ref, 1 - s, c * COMBINE_CHUNK + tt)
                rs = slice(c * COMBINE_CHUNK, (c + 1) * COMBINE_CHUNK)
                tw = tw_ref[rs, :]
                routed = tw[:, 0:1] * gbuf[s, 0, rs, :]
                for k in range(1, TOP_K):
                    routed = routed + tw[:, k:k + 1] * gbuf[s, k, rs, :]
                v = DN_ALPHA * h_ref[rs, :] + (routed + sh_ref[rs, :].astype(F32))
                o_ref[rs, :] = _layernorm(v, g, b)

            @pl.when(i == nsteps - 1)
            def _():
                wait_slabs(1 - s)


def _combine(dest3, ys, topw, h1, shared, g, b):
    n_tok = h1.shape[0]
    nsteps, _, tm = dest3.shape
    row = lambda i: (i, 0)
    fixed = lambda i: (0, 0)
    return pl.pallas_call(
        functools.partial(_combine_kernel, tm=tm),
        out_shape=jax.ShapeDtypeStruct((n_tok, D_MODEL), F32),
        grid=(nsteps,),
        in_specs=[pl.BlockSpec((1, TOP_K, tm), lambda i: (i, 0, 0), memory_space=pltpu.SMEM),
                  pl.BlockSpec((1, TOP_K, tm), lambda i: (jnp.minimum(i + 1, nsteps - 1), 0, 0),
                               memory_space=pltpu.SMEM),
                  pl.BlockSpec(memory_space=pl.ANY),
                  pl.BlockSpec((tm, TOP_K), row),
                  pl.BlockSpec((tm, D_MODEL), row), pl.BlockSpec((tm, D_MODEL), row),
                  pl.BlockSpec((1, D_MODEL), fixed), pl.BlockSpec((1, D_MODEL), fixed)],
        out_specs=pl.BlockSpec((tm, D_MODEL), row),
        scratch_shapes=[pltpu.VMEM((2, TOP_K, tm, D_MODEL), F32), pltpu.SemaphoreType.DMA((2,))],
        compiler_params=_cparams(("arbitrary",)),
        name="combine_ln2",
    )(dest3, dest3, ys, topw, h1, shared, g, b)


PLAN_TM = 128
LANES = 128
SUBLANES = 8


def _small_int_halves(v):
    hi = jnp.floor(v * (1.0 / 256.0))
    return hi.astype(BF16), (v - hi * 256.0).astype(BF16)


def _plan_kernel(ti_ref, dest_ref, info_ref, rank_s, cntc_s, cntr_s, pst_s, *, tm):
    p = pl.program_id(0)
    i = pl.program_id(1)
    ti = ti_ref[...]
    e_iota = lax.broadcasted_iota(I32, (N_EXPERTS, tm), 0)
    reps = tm // LANES

    @pl.when(jnp.logical_and(p == 0, i == 0))
    def _():
        cntc_s[...] = jnp.zeros(cntc_s.shape, F32)
        cntr_s[...] = jnp.zeros(cntr_s.shape, F32)

    @pl.when(p == 0)
    def _():
        hits = [e_iota == ti[k:k + 1] for k in range(TOP_K)]
        m = jnp.where(hits[0], 1.0, 0.0)
        for k in range(1, TOP_K):
            m = m + jnp.where(hits[k], 1.0, 0.0)
        mb = m.astype(BF16)
        earlier = lax.broadcasted_iota(I32, (tm, tm), 0) < lax.broadcasted_iota(I32, (tm, tm), 1)
        pfx = jnp.dot(mb, jnp.where(earlier, 1.0, 0.0).astype(BF16), preferred_element_type=F32)
        val = pfx + jnp.concatenate([cntc_s[...]] * reps, axis=1)
        rank_s[i] = jnp.concatenate(
            [jnp.sum(jnp.where(hits[k], val, 0.0), axis=0, keepdims=True) for k in range(TOP_K)], axis=0)
        cntc_s[...] += jnp.dot(mb, jnp.ones((tm, LANES), BF16), preferred_element_type=F32)
        cntr_s[...] += lax.dot_general(jnp.ones((8, tm), BF16), mb, (((1,), (1,)), ((), ())),
                                       preferred_element_type=F32)

    @pl.when(jnp.logical_and(p == 1, i == 0))
    def _():
        def n_blocks_of(cnt):
            return jnp.right_shift(cnt.astype(I32) + (EXPERT_BLOCK - 1), EXPERT_BLOCK.bit_length() - 1).astype(F32)

        ee0 = lax.broadcasted_iota(I32, (N_EXPERTS, N_EXPERTS), 0)
        ee1 = lax.broadcasted_iota(I32, (N_EXPERTS, N_EXPERTS), 1)
        hi, lo = _small_int_halves(n_blocks_of(cntc_s[...]))
        lower = jnp.where(ee1 < ee0, 1.0, 0.0).astype(BF16)
        bst_c = 256.0 * jnp.dot(lower, hi, preferred_element_type=F32) + jnp.dot(lower, lo, preferred_element_type=F32)
        pst_s[...] = bst_c * float(EXPERT_BLOCK)
        cnt_r = cntr_s[...]
        nb_r = n_blocks_of(cnt_r)
        hi, lo = _small_int_halves(nb_r)
        upper = jnp.where(ee0 < ee1, 1.0, 0.0).astype(BF16)
        bst_r = 256.0 * jnp.dot(hi, upper, preferred_element_type=F32) + jnp.dot(lo, upper, preferred_element_type=F32)
        sel = lax.broadcasted_iota(I32, (8, N_EXPERTS), 0)
        info = jnp.where(sel == 0, cnt_r, jnp.where(sel == 1, nb_r, jnp.where(sel == 2, bst_r, 0.0)))
        info_ref[...] = info.astype(I32)

    @pl.when(p == 1)
    def _():
        pst = jnp.concatenate([pst_s[...]] * reps, axis=1)
        off = jnp.concatenate(
            [jnp.sum(jnp.where(e_iota == ti[k:k + 1], pst, 0.0), axis=0, keepdims=True) for k in range(TOP_K)], axis=0)
        dest_ref[0] = (rank_s[i] + off).astype(I32)


def _plan(topi_t, tm=PLAN_TM):
    n_tok = topi_t.shape[1]
    nt = n_tok // tm
    return pl.pallas_call(
        functools.partial(_plan_kernel, tm=tm),
        out_shape=(jax.ShapeDtypeStruct((nt, TOP_K, tm), I32), jax.ShapeDtypeStruct((8, N_EXPERTS), I32)),
        grid=(2, nt),
        in_specs=[pl.BlockSpec((TOP_K, tm), lambda p, i: (0, i))],
        out_specs=(pl.BlockSpec((1, TOP_K, tm), lambda p, i: (i * p, 0, 0)),
                   pl.BlockSpec((8, N_EXPERTS), lambda p, i: (0, 0))),
        scratch_shapes=[pltpu.VMEM((nt, TOP_K, tm), F32), pltpu.VMEM((N_EXPERTS, LANES), F32),
                        pltpu.VMEM((8, N_EXPERTS), F32), pltpu.VMEM((N_EXPERTS, LANES), F32)],
        compiler_params=_cparams(("arbitrary", "arbitrary")),
        name="expert_row_plan",
    )(topi_t)


def _dispatch_kernel(cnt_ref, nblk_ref, bst_ref, dest_ref, hp_ref, hb_ref, wg_ref, wu_ref, wd_ref, xs_ref, sh_ref,
                     zrow, sem, zsem, gsem, tsem, *, tm, epg, tpg, n_blocks):
    i = pl.program_id(0)

    @pl.when(i == 0)
    def _():
        zrow[...] = jnp.zeros(zrow.shape, F32)

    nused = bst_ref[N_EXPERTS - 1] + nblk_ref[N_EXPERTS - 1]
    t_lo = jnp.minimum(nused + i * tpg, n_blocks)
    t_hi = jnp.minimum(nused + (i + 1) * tpg, n_blocks)

    def tail_copy(blk):
        return pltpu.make_async_copy(
            zrow, xs_ref.at[pl.ds(pl.multiple_of(blk * EXPERT_BLOCK, EXPERT_BLOCK), EXPERT_BLOCK), :], tsem)

    def tfill(blk, carry):
        tail_copy(blk).start()
        return carry

    def twait(blk, carry):
        tail_copy(blk).wait()
        return carry

    lax.fori_loop(t_lo, t_hi, tfill, 0)

    for t in range(tm):
        for k in range(TOP_K):
            pltpu.make_async_copy(hp_ref.at[pl.ds(t, 1), :], xs_ref.at[pl.ds(dest_ref[0, k, t], 1), :],
                                  sem).start(priority=k % 2)
    hb = hb_ref[...]
    hid = jax.nn.silu(jnp.dot(hb, wg_ref[...], preferred_element_type=F32)) * jnp.dot(
        hb, wu_ref[...], preferred_element_type=F32)
    sh_ref[...] = jnp.dot(hid.astype(BF16), wd_ref[...], preferred_element_type=F32).astype(BF16)

    def zfill(r, carry):
        pltpu.make_async_copy(zrow.at[pl.ds(0, 1), :], xs_ref.at[pl.ds(r, 1), :], zsem).start()
        return carry

    def zwait(r, carry):
        pltpu.make_async_copy(zrow.at[pl.ds(0, 1), :], xs_ref.at[pl.ds(0, 1), :], zsem).wait()
        return carry

    def group_copy(grp):
        return pltpu.make_async_copy(
            zrow.at[pl.ds(0, SUBLANES), :], xs_ref.at[pl.ds(pl.multiple_of(grp * SUBLANES, SUBLANES), SUBLANES), :],
            gsem)

    def gfill(grp, carry):
        group_copy(grp).start()
        return carry

    def gwait(grp, carry):
        group_copy(grp).wait()
        return carry

    bounds = []
    for q in range(epg):
        e = i * epg + q
        lo = bst_ref[e] * EXPERT_BLOCK + cnt_ref[e]
        hi = (bst_ref[e] + nblk_ref[e]) * EXPERT_BLOCK
        mid = jnp.minimum(jnp.bitwise_and(lo + (SUBLANES - 1), -SUBLANES), hi)
        g_lo = jnp.right_shift(mid, SUBLANES.bit_length() - 1)
        g_hi = jnp.right_shift(hi, SUBLANES.bit_length() - 1)
        bounds.append((lo, mid, g_lo, g_hi))
        lax.fori_loop(lo, mid, zfill, 0)
        lax.fori_loop(g_lo, g_hi, gfill, 0)

    for k in range(TOP_K):
        pltpu.make_async_copy(hp_ref, xs_ref.at[pl.ds(0, tm), :], sem).wait()
    for lo, mid, g_lo, g_hi in bounds:
        lax.fori_loop(lo, mid, zwait, 0)
        lax.fori_loop(g_lo, g_hi, gwait, 0)
    lax.fori_loop(t_lo, t_hi, twait, 0)


def _dispatch(cnt, nblk, bst, dest3, hp, hb, wgs, wus, wds, n_rows):
    nsteps, _, tm = dest3.shape
    n_tok = hp.shape[0]
    n_blocks = n_rows // EXPERT_BLOCK
    assert N_EXPERTS % nsteps == 0
    row = lambda i, *_: (i, 0)
    fixed = lambda i, *_: (0, 0)
    grid_spec = pltpu.PrefetchScalarGridSpec(
        num_scalar_prefetch=3, grid=(nsteps,),
        in_specs=[pl.BlockSpec((1, TOP_K, tm), lambda i, *_: (i, 0, 0), memory_space=pltpu.SMEM),
                  pl.BlockSpec((tm, D_MODEL), row), pl.BlockSpec((tm, D_MODEL), row),
                  pl.BlockSpec((D_MODEL, D_EXPERT), fixed), pl.BlockSpec((D_MODEL, D_EXPERT), fixed),
                  pl.BlockSpec((D_EXPERT, D_MODEL), fixed)],
        out_specs=(pl.BlockSpec(memory_space=pl.ANY), pl.BlockSpec((tm, D_MODEL), row)),
        scratch_shapes=[pltpu.VMEM((EXPERT_BLOCK, D_MODEL), F32), pltpu.SemaphoreType.DMA(()),
                        pltpu.SemaphoreType.DMA(()), pltpu.SemaphoreType.DMA(()), pltpu.SemaphoreType.DMA(())])
    return pl.pallas_call(
        functools.partial(_dispatch_kernel, tm=tm, epg=N_EXPERTS // nsteps, tpg=-(-n_blocks // nsteps),
                          n_blocks=n_blocks),
        out_shape=(jax.ShapeDtypeStruct((n_rows, D_MODEL), F32), jax.ShapeDtypeStruct((n_tok, D_MODEL), BF16)),
        grid_spec=grid_spec,
        compiler_params=_cparams(("arbitrary",)),
        name="dispatch_shared",
    )(cnt, nblk, bst, dest3, hp, hb, wgs, wus, wds)


def _rope_tables(n_pos):
    half = ROT_DIM // 2
    inv_freq = ROPE_THETA ** (-jnp.arange(half, dtype=F32) / half)
    ang = jnp.arange(n_pos).astype(F32)[:, None] * inv_freq[None, :]
    cos, sin = jnp.cos(ang), jnp.sin(ang)
    c = jnp.concatenate([cos, cos, jnp.ones((n_pos, HEAD_DIM - ROT_DIM), F32)], axis=1)
    s = jnp.concatenate([sin, sin, jnp.zeros((n_pos, HEAD_DIM - ROT_DIM), F32)], axis=1)
    return c, s


def kernel(x, meta_tokens, w_in, conv_w, conv_b, w_rg_a, b_rg_a, w_rg_x, b_rg_x, rg_lambda, attn_sinks, w_o_attn, w_o_lru, w_out, ln1_g, ln1_b, w_router, router_bias, w_gate_e, w_up_e, w_down_e, w_gate_s, w_up_s, w_down_s, ln2_g, ln2_b):
    batch, seq, _ = x.shape
    n_tok = batch * seq
    assert w_in.shape[0] == 1 and seq % 512 == 0 and meta_tokens.shape[0] == N_META
    x2 = x.reshape(n_tok, D_MODEL)
    row = lambda v: v.reshape(1, -1).astype(F32)

    wax = jnp.concatenate([w_rg_a[0], w_rg_x[0]], axis=-1).astype(BF16)
    c_t, s_t = _rope_tables(N_META + seq)
    tabs_meta = (c_t[:N_META], s_t[:N_META])
    tabs_real = (c_t[N_META:], s_t[N_META:])

    projm = _matmul(meta_tokens, w_in[0], BF16, N_META, 1024)
    lru_args = (conv_w[0], row(conv_b[0]), wax, row(b_rg_a[0]), row(b_rg_x[0]), row(rg_lambda[0]))
    _, h_meta = _lru(projm, jnp.zeros((N_META, D_LRU), BF16), jnp.zeros((1, D_LRU), F32), *lru_args,
                     batch=1, seq=N_META, tc=N_META)

    proj = _matmul(x2, w_in[0], BF16, 1024, 1024)
    o_attn = _attention(proj, projm, attn_sinks[0].astype(F32), tabs_real, tabs_meta, batch, seq)
    y_lru, _ = _lru(proj, projm[:, XR_COL:XR_COL + D_LRU], h_meta[0, 7:8], *lru_args, batch=batch, seq=seq, tc=LRU_SUB)
    z = _merge(o_attn, y_lru, w_o_attn[0].astype(BF16), w_o_lru[0].astype(BF16), proj)
    h1, h1b = _ln1(x2, z, w_out[0].astype(BF16), row(ln1_g[0]), row(ln1_b[0]))

    topi_t, topw_t = _router(h1b, w_router[0].T.astype(BF16), router_bias[0].reshape(N_EXPERTS, 1).astype(F32))
    dest3, info = _plan(topi_t)
    cnt, nblk, bst = info[0], info[1], info[2]
    n_rows = -(-(n_tok * TOP_K + N_EXPERTS * (EXPERT_BLOCK - 1)) // EXPERT_BLOCK) * EXPERT_BLOCK
    xs, shared = _dispatch(cnt, nblk, bst, dest3, h1, h1b,w_gate_s[0].astype(BF16), w_up_s[0].astype(BF16),
                           w_down_s[0].astype(BF16), n_rows)
    ys = _experts(nblk, bst, xs, w_gate_e[0], w_up_e[0], w_down_e[0])
    out = _combine(dest3, ys, topw_t.T, h1, shared, row(ln2_g[0]), row(ln2_b[0]))
    return out.reshape(batch, seq, D_MODEL)
```
